```python
import math
import jax
import jax.numpy as jnp
from jax import lax
import numpy as np

D_MODEL = 2048
BATCH = 4
SEQ = 8192
DEPTH = 4

CONV_WIDTH = 4
CHUNK = 64
N_BRANCH = 3
BRANCH_WIDTH = D_MODEL // 2
LRU_WIDTH = BRANCH_WIDTH
LRU_BLOCKS = 16
LRU_BLOCK_DIM = LRU_WIDTH // LRU_BLOCKS
LRU_C = 8.0
GLA_HEADS = 4
GLA_DK = D_MODEL // 16
GLA_DV = BRANCH_WIDTH // GLA_HEADS
GLA_LOWRANK = 16
GLA_TAU = 16.0
DN_HEADS = 8
DN_DK = D_MODEL // 16
DN_DV = BRANCH_WIDTH // DN_HEADS
D_FF = 4 * D_MODEL
DEEPNORM_ALPHA = (2.0 * DEPTH) ** 0.25
DEEPNORM_BETA = (8.0 * DEPTH) ** -0.25
LN_EPS = 1e-5
NORM_EPS = 1e-6
IN_SPLITS = (
    LRU_WIDTH,
    LRU_WIDTH,
    GLA_HEADS * GLA_DK,
    GLA_HEADS * GLA_DK,
    GLA_HEADS * GLA_DV,
    GLA_LOWRANK,
    GLA_HEADS * GLA_DV,
    DN_HEADS * (2 * DN_DK + DN_DV),
    DN_HEADS,
    DN_HEADS,
    DN_HEADS * DN_DV,
    N_BRANCH * D_MODEL,
)
D_IN = sum(IN_SPLITS)

kernel_name = 'hybrid_rglru_gla_gdn_deepnorm'


def _split_points(sizes):
    pts, acc = [], 0
    for s in sizes[:-1]:
        acc += s
        pts.append(acc)
    return pts


def _layernorm(x, g, b):
    xf = x.astype(jnp.float32)
    mu = jnp.mean(xf, axis=-1, keepdims=True)
    var = jnp.mean(jnp.square(xf - mu), axis=-1, keepdims=True)
    return ((xf - mu) * lax.rsqrt(var + LN_EPS) * g + b).astype(x.dtype)


def _rmsnorm(x, g):
    xf = x.astype(jnp.float32)
    return xf * lax.rsqrt(jnp.mean(xf * xf, axis=-1, keepdims=True) + NORM_EPS) * g.astype(jnp.float32)


def _l2norm(x):
    return x * lax.rsqrt(jnp.sum(x * x, axis=-1, keepdims=True) + NORM_EPS)


def _causal_conv(x, w):
    c = x.shape[-1]
    return lax.conv_general_dilated(
        x, w.astype(x.dtype)[:, None, :], window_strides=(1,),
        padding=[(CONV_WIDTH - 1, 0)], dimension_numbers=('NWC', 'WIO', 'NWC'),
        feature_group_count=c)


def _rg_lru_branch(xb, yb, conv_w, conv_b, wa, ba, wi, bi, lam):
    f32 = jnp.float32
    bsz, seq, _ = xb.shape
    xc = _causal_conv(xb.astype(f32), conv_w) + conv_b.astype(f32)
    xg = xc.reshape(bsz, seq, LRU_BLOCKS, LRU_BLOCK_DIM)
    r = jax.nn.sigmoid(jnp.einsum('bsgi,gij->bsgj', xg, wa.astype(f32)).reshape(bsz, seq, LRU_WIDTH) + ba)
    i = jax.nn.sigmoid(jnp.einsum('bsgi,gij->bsgj', xg, wi.astype(f32)).reshape(bsz, seq, LRU_WIDTH) + bi)
    log_a = -LRU_C * r * jax.nn.softplus(-lam.astype(f32))
    a = jnp.exp(log_a)
    u = jnp.sqrt(-jnp.expm1(2.0 * log_a)) * (i * xc)

    def combine(c1, c2):
        a1, b1 = c1
        a2, b2 = c2
        return a1 * a2, a2 * b1 + b2

    _, h = lax.associative_scan(combine, (a, u), axis=1)
    return h * jax.nn.gelu(yb.astype(f32))


def _gla_branch(q, k, v, alr, r, wa2, ba2, norm_g):
    f32 = jnp.float32
    bsz, seq, _ = q.shape
    n = seq // CHUNK
    shp_k = (bsz, n, CHUNK, GLA_HEADS, GLA_DK)
    shp_v = (bsz, n, CHUNK, GLA_HEADS, GLA_DV)
    gk = jax.nn.log_sigmoid(alr.astype(f32) @ wa2.astype(f32) + ba2) / GLA_TAU
    b = jnp.cumsum(gk.reshape(shp_k), axis=2)
    b_last = b[:, :, -1:]
    qc = q.astype(f32).reshape(shp_k) * (GLA_DK ** -0.5)
    kc = k.astype(f32).reshape(shp_k)
    vc = v.astype(f32).reshape(shp_v)
    qe = qc * jnp.exp(b)
    ke = kc * jnp.exp(-b)
    kd = kc * jnp.exp(b_last - b)
    causal = jnp.tril(jnp.ones((CHUNK, CHUNK), dtype=bool))
    att = jnp.where(causal, jnp.einsum('bnihd,bnjhd->bnhij', qe, ke), 0.0)
    o_intra = jnp.einsum('bnhij,bnjhv->bnihv', att, vc)
    upd = jnp.einsum('bnjhd,bnjhv->bnhdv', kd, vc)
    dec = jnp.exp(b_last[:, :, 0])

    def step(state, inp):
        d, u = inp
        return d[..., None] * state + u, state

    s0 = jnp.zeros((bsz, GLA_HEADS, GLA_DK, GLA_DV), f32)
    _, s_prev = lax.scan(step, s0, (jnp.moveaxis(dec, 1, 0), jnp.moveaxis(upd, 1, 0)))
    s_prev = jnp.moveaxis(s_prev, 0, 1)
    o = o_intra + jnp.einsum('bnihd,bnhdv->bnihv', qe, s_prev)
    o = _rmsnorm(o.reshape(bsz, seq, GLA_HEADS, GLA_DV), norm_g).reshape(bsz, seq, GLA_HEADS * GLA_DV)
    return o * jax.nn.silu(r.astype(f32))


def _gated_deltanet_branch(qkv, beta_logit, a_logit, z, conv_w, a_log, dt_bias, norm_g):
    f32 = jnp.float32
    bsz, seq, _ = qkv.shape
    n = seq // CHUNK
    qkv = jax.nn.silu(_causal_conv(qkv.astype(f32), conv_w))
    q, k, v = jnp.split(qkv, [DN_HEADS * DN_DK, 2 * DN_HEADS * DN_DK], axis=-1)
    q = _l2norm(q.reshape(bsz, seq, DN_HEADS, DN_DK)) * (DN_DK ** -0.5)
    k = _l2norm(k.reshape(bsz, seq, DN_HEADS, DN_DK))
    v = v.reshape(bsz, seq, DN_HEADS, DN_DV)
    beta = jax.nn.sigmoid(beta_logit.astype(f32))
    g = -jnp.exp(a_log.astype(f32)) * jax.nn.softplus(a_logit.astype(f32) + dt_bias)

    def chunks(t):
        return t.reshape(bsz, n, CHUNK, DN_HEADS, -1).transpose(0, 1, 3, 2, 4)

    qc, kc, vc = chunks(q), chunks(k), chunks(v)
    bc = chunks(beta[..., None])[..., 0]
    gc = jnp.cumsum(chunks(g[..., None])[..., 0], axis=-1)
    incl = jnp.tril(jnp.ones((CHUNK, CHUNK), dtype=bool))
    diff = gc[..., :, None] - gc[..., None, :]
    decay = jnp.where(incl, jnp.exp(jnp.where(incl, diff, 0.0)), 0.0)
    kb = kc * bc[..., None]
    lower = jnp.einsum('bnhid,bnhjd->bnhij', kb, kc) * decay
    rhs = jnp.concatenate([vc * bc[..., None], kb * jnp.exp(gc)[..., None]], axis=-1)
    sol = lax.linalg.triangular_solve(lower, rhs, left_side=True, lower=True, unit_diagonal=True)
    value, kcum = sol[..., :DN_DV], sol[..., DN_DV:]
    aqk = jnp.einsum('bnhid,bnhjd->bnhij', qc, kc) * decay
    qg = qc * jnp.exp(gc)[..., None]
    g_last = gc[..., -1]
    kg = kc * jnp.exp(g_last[..., None] - gc)[..., None]

    def step(state, inp):
        val, kcd, qgn, kgn, aq, gl = inp
        v_new = val - jnp.einsum('bhcd,bhdv->bhcv', kcd, state)
        o = jnp.einsum('bhcd,bhdv->bhcv', qgn, state) + jnp.einsum('bhij,bhjv->bhiv', aq, v_new)
        state = state * jnp.exp(gl)[..., None, None] + jnp.einsum('bhcd,bhcv->bhdv', kgn, v_new)
        return state, o

    xs = tuple(jnp.moveaxis(t, 1, 0) for t in (value, kcum, qg, kg, aqk, g_last))
    s0 = jnp.zeros((bsz, DN_HEADS, DN_DK, DN_DV), f32)
    _, o = lax.scan(step, s0, xs)
    o = o.transpose(1, 0, 3, 2, 4).reshape(bsz, seq, DN_HEADS, DN_DV)
    o = _rmsnorm(o, norm_g) * jax.nn.silu(z.astype(f32).reshape(bsz, seq, DN_HEADS, DN_DV))
    return o.reshape(bsz, seq, DN_HEADS * DN_DV)


def _token_mixer(x, w_in, lru_conv_w, lru_conv_b, lru_wa, lru_ba, lru_wi, lru_bi, lru_lambda,
                 gla_wa2, gla_ba2, gla_norm_g, dn_conv_w, dn_a_log, dn_dt_bias, dn_norm_g,
                 w_branch, b_gate, w_out):
    bsz, seq, _ = x.shape
    proj = jnp.einsum('bsd,de->bse', x, w_in)
    (lru_x, lru_y, gla_q, gla_k, gla_v, gla_alr, gla_r,
     dn_qkv, dn_b, dn_a, dn_z, gate_logits) = jnp.split(proj, _split_points(IN_SPLITS), axis=-1)
    y_lru = _rg_lru_branch(lru_x, lru_y, lru_conv_w, lru_conv_b, lru_wa, lru_ba, lru_wi, lru_bi,
                           lru_lambda).astype(x.dtype)
    y_gla = _gla_branch(gla_q, gla_k, gla_v, gla_alr, gla_r, gla_wa2, gla_ba2, gla_norm_g).astype(x.dtype)
    y_dn = _gated_deltanet_branch(dn_qkv, dn_b, dn_a, dn_z, dn_conv_w, dn_a_log, dn_dt_bias,
                                  dn_norm_g).astype(x.dtype)
    gates = jax.nn.sigmoid(gate_logits.reshape(bsz, seq, N_BRANCH, D_MODEL) + b_gate)
    merged = (gates[:, :, 0] * (y_lru @ w_branch[0])
              + gates[:, :, 1] * (y_gla @ w_branch[1])
              + gates[:, :, 2] * (y_dn @ w_branch[2]))
    return merged @ w_out


def _squared_relu_mlp(x, w1, b1, w2, b2):
    return jnp.square(jax.nn.relu(x @ w1 + b1)) @ w2 + b2


def setup_inputs(seed: int = 0) -> dict:
    key = jax.random.key(seed)
    ks = jax.random.split(key, 28)
    f32 = jnp.float32
    L = DEPTH

    def nrm(k, shape, scale):
        return jax.random.normal(k, shape, f32) * scale

    lru_a0 = jax.random.uniform(ks[8], (L, LRU_WIDTH), f32, 0.9, 0.999)
    lru_sig = lru_a0 ** (1.0 / LRU_C)
    dt0 = jnp.exp(jax.random.uniform(ks[13], (L, DN_HEADS), f32, math.log(1e-3), math.log(1e-1)))
    return {
        'x': nrm(ks[0], (BATCH, SEQ, D_MODEL), 1.0),
        'w_in': nrm(ks[1], (L, D_MODEL, D_IN), D_MODEL ** -0.5),
        'lru_conv_w': nrm(ks[2], (L, CONV_WIDTH, LRU_WIDTH), CONV_WIDTH ** -0.5),
        'lru_conv_b': nrm(ks[3], (L, LRU_WIDTH), 0.01),
        'lru_wa': nrm(ks[4], (L, LRU_BLOCKS, LRU_BLOCK_DIM, LRU_BLOCK_DIM), LRU_BLOCK_DIM ** -0.5),
        'lru_ba': nrm(ks[5], (L, LRU_WIDTH), 0.01),
        'lru_wi': nrm(ks[6], (L, LRU_BLOCKS, LRU_BLOCK_DIM, LRU_BLOCK_DIM), LRU_BLOCK_DIM ** -0.5),
        'lru_bi': nrm(ks[7], (L, LRU_WIDTH), 0.01),
        'lru_lambda': jnp.log(lru_sig) - jnp.log1p(-lru_sig),
        'gla_wa2': nrm(ks[9], (L, GLA_LOWRANK, GLA_HEADS * GLA_DK), GLA_LOWRANK ** -0.5),
        'gla_ba2': nrm(ks[10], (L, GLA_HEADS * GLA_DK), 0.01),
        'gla_norm_g': 1.0 + nrm(ks[11], (L, GLA_DV), 0.01),
        'dn_conv_w': nrm(ks[12], (L, CONV_WIDTH, DN_HEADS * (2 * DN_DK + DN_DV)), CONV_WIDTH ** -0.5),
        'dn_a_log': jnp.log(jax.random.uniform(ks[14], (L, DN_HEADS), f32, 1.0, 16.0)),
        'dn_dt_bias': dt0 + jnp.log(-jnp.expm1(-dt0)),
        'dn_norm_g': 1.0 + nrm(ks[15], (L, DN_DV), 0.01),
        'w_branch': nrm(ks[16], (L, N_BRANCH, BRANCH_WIDTH, D_MODEL), DEEPNORM_BETA * BRANCH_WIDTH ** -0.5),
        'b_gate': nrm(ks[17], (L, N_BRANCH, D_MODEL), 0.01),
        'w_out': nrm(ks[18], (L, D_MODEL, D_MODEL), DEEPNORM_BETA * D_MODEL ** -0.5),
        'ln1_g': 1.0 + nrm(ks[19], (L, D_MODEL), 0.01),
        'ln1_b': nrm(ks[20], (L, D_MODEL), 0.01),
        'mlp_w1': nrm(ks[21], (L, D_MODEL, D_FF), DEEPNORM_BETA * D_MODEL ** -0.5),
        'mlp_b1': nrm(ks[22], (L, D_FF), 0.01),
        'mlp_w2': nrm(ks[23], (L, D_FF, D_MODEL), DEEPNORM_BETA * D_FF ** -0.5),
        'mlp_b2': nrm(ks[24], (L, D_MODEL), 0.01),
        'ln2_g': 1.0 + nrm(ks[25], (L, D_MODEL), 0.01),
        'ln2_b': nrm(ks[26], (L, D_MODEL), 0.01),
    }


def reference(x, w_in, lru_conv_w, lru_conv_b, lru_wa, lru_ba, lru_wi, lru_bi, lru_lambda,
              gla_wa2, gla_ba2, gla_norm_g, dn_conv_w, dn_a_log, dn_dt_bias, dn_norm_g,
              w_branch, b_gate, w_out, ln1_g, ln1_b, mlp_w1, mlp_b1, mlp_w2, mlp_b2,
              ln2_g, ln2_b):
    for l in range(DEPTH):
        m = _token_mixer(x, w_in[l], lru_conv_w[l], lru_conv_b[l], lru_wa[l], lru_ba[l], lru_wi[l],
                         lru_bi[l], lru_lambda[l], gla_wa2[l], gla_ba2[l], gla_norm_g[l],
                         dn_conv_w[l], dn_a_log[l], dn_dt_bias[l], dn_norm_g[l],
                         w_branch[l], b_gate[l], w_out[l])
        x = _layernorm(DEEPNORM_ALPHA * x + m, ln1_g[l], ln1_b[l])
        h = _squared_relu_mlp(x, mlp_w1[l], mlp_b1[l], mlp_w2[l], mlp_b2[l])
        x = _layernorm(DEEPNORM_ALPHA * x + h, ln2_g[l], ln2_b[l])
    return x
```

```python
import functools

import jax
import jax.numpy as jnp
from jax import lax
from jax.experimental import pallas as pl
from jax.experimental.pallas import tpu as pltpu

F32 = jnp.float32
BF16 = jnp.bfloat16

CONV_WIDTH = 4
CHUNK = 64
LRU_C = 8.0
GLA_TAU = 16.0
LN_EPS = 1e-5
NORM_EPS = 1e-6

SUBLANES = 8
LANES = 128
VMEM_LIMIT_BYTES = 56 * 1024 * 1024

_NT = (((1,), (1,)), ((), ()))
_TN = (((0,), (0,)), ((), ()))


def _params(*sem):
    return pltpu.CompilerParams(dimension_semantics=sem, vmem_limit_bytes=VMEM_LIMIT_BYTES)


def _sigmoid(x):
    return 1.0 / (1.0 + jnp.exp(-x))


def _softplus(x):
    return jnp.maximum(x, 0.0) + jnp.log1p(jnp.exp(-jnp.abs(x)))


def _silu(x):
    return x * _sigmoid(x)


def _gelu_tanh(x):
    return x * (0.5 * (1.0 + jnp.tanh(0.7978845608028654 * (x + 0.044715 * (x * x * x)))))


def _dot(a, b):
    return jnp.dot(a, b, preferred_element_type=F32)


def _group_cumsum(x, period):
    pos = lax.broadcasted_iota(jnp.int32, x.shape, 0) & (period - 1)
    k = 1
    while k < period:
        x = x + jnp.where(pos >= k, pltpu.roll(x, k, 0), 0.0)
        k *= 2
    return x


def _mm_kernel(x_ref, w_ref, o_ref):
    o_ref[...] = _dot(x_ref[...], w_ref[...]).astype(o_ref.dtype)


def _matmul(x, w, out_dtype, tm, tn, name):
    m, k = x.shape
    n = w.shape[1]
    assert m % tm == 0 and n % tn == 0
    return pl.pallas_call(
        _mm_kernel,
        grid=(n // tn, m // tm),
        in_specs=[pl.BlockSpec((tm, k), lambda j, i: (i, 0)),
                  pl.BlockSpec((k, tn), lambda j, i: (0, j))],
        out_specs=pl.BlockSpec((tm, tn), lambda j, i: (i, j)),
        out_shape=jax.ShapeDtypeStruct((m, n), out_dtype),
        compiler_params=_params("parallel", "parallel"),
        name=name,
    )(x, w)


def _load_conv_tile(x_ref, xbuf, t, tile):
    c = xbuf.shape[1]

    @pl.when(t == 0)
    def _():
        xbuf[0:SUBLANES, :] = jnp.zeros((SUBLANES, c), F32)

    @pl.when(t > 0)
    def _():
        xbuf[0:SUBLANES, :] = xbuf[tile:tile + SUBLANES, :]

    xbuf[SUBLANES:SUBLANES + tile, :] = x_ref[...].astype(F32)


def _conv_rows(xbuf, cw_ref, start, rows):
    acc = None
    for k in range(CONV_WIDTH):
        off = SUBLANES - (CONV_WIDTH - 1) + k + start
        term = cw_ref[k:k + 1, :] * xbuf[off:off + rows, :]
        acc = term if acc is None else acc + term
    return acc


def _lru_kernel(x_ref, y_ref, cw_ref, cb_ref, wa_ref, ba_ref, wi_ref, bi_ref, lam_ref,
                o_ref, xbuf, hcar, *, tile, sub):
    t = pl.program_id(1)
    c = xbuf.shape[1]
    _load_conv_tile(x_ref, xbuf, t, tile)

    @pl.when(t == 0)
    def _():
        hcar[...] = jnp.zeros(hcar.shape, F32)

    neg_c_sp = -LRU_C * _softplus(-lam_ref[...])
    pos = lax.broadcasted_iota(jnp.int32, (sub, c), 0) & (SUBLANES - 1)
    h = hcar[0:1, :]
    for s in range(tile // sub):
        xc = _conv_rows(xbuf, cw_ref, s * sub, sub) + cb_ref[...]
        xcb = xc.astype(BF16)
        r = _sigmoid(_dot(xcb, wa_ref[...]) + ba_ref[...])
        i = _sigmoid(_dot(xcb, wi_ref[...]) + bi_ref[...])
        log_a = neg_c_sp * r
        a = jnp.exp(log_a)
        u = jnp.sqrt(-jnp.tanh(log_a) * (a * a + 1.0)) * (i * xc)
        k = 1
        while k < SUBLANES:
            m = pos >= k
            u = jnp.where(m, a * pltpu.roll(u, k, 0) + u, u)
            a = jnp.where(m, a * pltpu.roll(a, k, 0), a)
            k *= 2
        outs = []
        for g in range(sub // SUBLANES):
            hg = a[g * SUBLANES:(g + 1) * SUBLANES] * h + u[g * SUBLANES:(g + 1) * SUBLANES]
            h = hg[SUBLANES - 1:SUBLANES]
            outs.append(hg)
        hs = jnp.concatenate(outs, axis=0)
        yv = y_ref[s * sub:(s + 1) * sub, :].astype(F32)
        o_ref[s * sub:(s + 1) * sub, :] = (hs * _gelu_tanh(yv)).astype(o_ref.dtype)
    hcar[...] = jnp.broadcast_to(h, hcar.shape)


def _lru_branch(proj, cols, bsz, seq, conv_w, conv_b, wa_full, ba, wi_full, bi, lam, tile, sub):
    c = conv_w.shape[1]
    nt = seq // tile
    xcol, ycol = cols["lru_x"] // c, cols["lru_y"] // c
    full = lambda b, t: (0, 0)
    return pl.pallas_call(
        functools.partial(_lru_kernel, tile=tile, sub=sub),
        grid=(bsz, nt),
        in_specs=[pl.BlockSpec((tile, c), lambda b, t: (b * nt + t, xcol)),
                  pl.BlockSpec((tile, c), lambda b, t: (b * nt + t, ycol)),
                  pl.BlockSpec((CONV_WIDTH, c), full),
                  pl.BlockSpec((1, c), full),
                  pl.BlockSpec((c, c), full),
                  pl.BlockSpec((1, c), full),
                  pl.BlockSpec((c, c), full),
                  pl.BlockSpec((1, c), full),
                  pl.BlockSpec((1, c), full)],
        out_specs=pl.BlockSpec((tile, c), lambda b, t: (b * nt + t, 0)),
        out_shape=jax.ShapeDtypeStruct((bsz * seq, c), BF16),
        scratch_shapes=[pltpu.VMEM((tile + SUBLANES, c), F32), pltpu.VMEM((SUBLANES, c), F32)],
        compiler_params=_params("parallel", "arbitrary"),
        name="rg_lru",
    )(proj, proj, conv_w, conv_b, wa_full, ba, wi_full, bi, lam)


def _gla_kernel(q_ref, k_ref, v_ref, r_ref, sm_ref, wa2_ref, ba2_ref, ng_ref, o_ref, st_ref,
                *, tile, heads, dk, dv):
    t = pl.program_id(1)

    @pl.when(t == 0)
    def _():
        st_ref[...] = jnp.zeros(st_ref.shape, F32)

    z = _dot(sm_ref[...].astype(BF16), wa2_ref[...]) + ba2_ref[...]
    gk = (jnp.minimum(z, 0.0) - jnp.log1p(jnp.exp(-jnp.abs(z)))) / GLA_TAU
    b = _group_cumsum(gk, CHUNK)
    row = lax.broadcasted_iota(jnp.int32, (CHUNK, CHUNK), 0)
    col = lax.broadcasted_iota(jnp.int32, (CHUNK, CHUNK), 1)
    causal = row >= col
    scale = dk ** -0.5
    ng = ng_ref[...]
    for c in range(tile // CHUNK):
        rows = slice(c * CHUNK, (c + 1) * CHUNK)
        bc = b[rows]
        bl = bc[CHUNK - 1:CHUNK]
        qf = q_ref[rows, :].astype(F32)
        kf = k_ref[rows, :].astype(F32)
        qe = ((qf * scale) * jnp.exp(bc)).astype(BF16)
        ke = (kf * jnp.exp(-bc)).astype(BF16)
        kd = (kf * jnp.exp(bl - bc)).astype(BF16)
        dec = jnp.exp(bl)
        for h in range(heads):
            ks = slice(h * dk, (h + 1) * dk)
            vs = slice(h * dv, (h + 1) * dv)
            vh = v_ref[rows, vs]
            att = lax.dot_general(qe[:, ks], ke[:, ks], _NT, preferred_element_type=F32)
            att = jnp.where(causal, att, 0.0)
            st = st_ref[h]
            o = _dot(att.astype(BF16), vh) + lax.dot_general(
                qe[:, ks], st.astype(BF16), _NT, preferred_element_type=F32)
            st_ref[h] = st * dec[:, ks] + lax.dot_general(
                vh, kd[:, ks], _TN, preferred_element_type=F32)
            o = o * lax.rsqrt(jnp.mean(o * o, axis=-1, keepdims=True) + NORM_EPS) * ng
            o_ref[rows, vs] = (o * _silu(r_ref[rows, vs].astype(F32))).astype(o_ref.dtype)


def _gla_branch(proj, small, cols, bsz, seq, wa2_pad, ba2, norm_g, tile):
    hdk = wa2_pad.shape[1]
    dv = norm_g.shape[1]
    heads = 4
    dk = hdk // heads
    hdv = heads * dv
    nt = seq // tile
    full = lambda b, t: (0, 0)
    qcol, kcol = cols["gla_q"] // hdk, cols["gla_k"] // hdk
    vcol, rcol = cols["gla_v"] // hdv, cols["gla_r"] // hdv
    return pl.pallas_call(
        functools.partial(_gla_kernel, tile=tile, heads=heads, dk=dk, dv=dv),
        grid=(bsz, nt),
        in_specs=[pl.BlockSpec((tile, hdk), lambda b, t: (b * nt + t, qcol)),
                  pl.BlockSpec((tile, hdk), lambda b, t: (b * nt + t, kcol)),
                  pl.BlockSpec((tile, hdv), lambda b, t: (b * nt + t, vcol)),
                  pl.BlockSpec((tile, hdv), lambda b, t: (b * nt + t, rcol)),
                  pl.BlockSpec((tile, LANES), lambda b, t: (b * nt + t, 0)),
                  pl.BlockSpec((LANES, hdk), full),
                  pl.BlockSpec((1, hdk), full),
                  pl.BlockSpec((1, dv), full)],
        out_specs=pl.BlockSpec((tile, hdv), lambda b, t: (b * nt + t, 0)),
        out_shape=jax.ShapeDtypeStruct((bsz * seq, hdv), BF16),
        scratch_shapes=[pltpu.VMEM((heads, dv, dk), F32)],
        compiler_params=_params("parallel", "arbitrary"),
        name="gla",
    )(proj, proj, proj, proj, small, wa2_pad, ba2, norm_g)


def _unit_lower_inverse(a, row, col):
    hi = lax.Precision.HIGHEST
    dot = lambda x, y: jnp.dot(x, y, preferred_element_type=F32, precision=hi)
    eye = (row == col).astype(F32)
    blk16 = (row // 16) == (col // 16)
    blk32 = (row // 32) == (col // 32)
    d = jnp.where(blk16, a, 0.0)
    d2 = dot(d, d)
    d4 = dot(d2, d2)
    d8 = dot(d4, d4)
    t = dot(dot(dot(eye - d, eye + d2), eye + d4), eye + d8)
    c1 = jnp.where(jnp.logical_and(blk32, jnp.logical_not(blk16)), a, 0.0)
    t = t - dot(dot(t, c1), t)
    c2 = jnp.where(blk32, 0.0, a)
    t = t - dot(dot(t, c2), t)
    return t


def _dn_kernel(qkv_ref, z_ref, sm_ref, cw_ref, alog_ref, dtb_ref, ng_ref, o_ref,
               xbuf, act, gsc, bsc, st_ref, *, tile, heads, dk, dv, b_lane, a_lane):
    t = pl.program_id(1)
    _load_conv_tile(qkv_ref, xbuf, t, tile)

    @pl.when(t == 0)
    def _():
        st_ref[...] = jnp.zeros(st_ref.shape, F32)

    hk = heads * dk
    for h in range(heads):
        for part, scale in ((0, dk ** -0.5), (1, 1.0)):
            cs = slice(part * hk + h * dk, part * hk + (h + 1) * dk)
            x = _silu(_conv_cols(xbuf, cw_ref, cs, tile))
            x = x * lax.rsqrt(jnp.sum(x * x, axis=-1, keepdims=True) + NORM_EPS)
            act[:, cs] = x * scale if part == 0 else x
    vs_all = slice(2 * hk, 2 * hk + heads * dv)
    act[:, vs_all] = _silu(_conv_cols(xbuf, cw_ref, vs_all, tile))

    sm = sm_ref[...]
    bsc[...] = _sigmoid(sm)
    gsc[...] = _group_cumsum(-jnp.exp(alog_ref[...]) * _softplus(sm + dtb_ref[...]), CHUNK)

    row = lax.broadcasted_iota(jnp.int32, (CHUNK, CHUNK), 0)
    col = lax.broadcasted_iota(jnp.int32, (CHUNK, CHUNK), 1)
    incl = row >= col
    strict = row > col
    ng = ng_ref[...]
    hi = lax.Precision.HIGHEST

    def chunk_body(c, carry):
        r0 = pl.multiple_of(c * CHUNK, CHUNK)
        rows = pl.ds(r0, CHUNK)
        gc = gsc[rows, :]
        be = bsc[rows, :]
        gl = gc[CHUNK - 1:CHUNK, :]
        eg = jnp.exp(gc)
        egl = jnp.exp(gl - gc)
        edl = jnp.exp(gl)
        gct = gc.T
        for h in range(heads):
            q = act[rows, h * dk:(h + 1) * dk]
            k = act[rows, hk + h * dk:hk + (h + 1) * dk]
            v = act[rows, 2 * hk + h * dv:2 * hk + (h + 1) * dv]
            la = a_lane + h
            bcol = be[:, b_lane + h:b_lane + h + 1]
            diff = gc[:, la:la + 1] - gct[la:la + 1, :]
            decay = jnp.where(incl, jnp.exp(jnp.where(incl, diff, 0.0)), 0.0)
            kb = k * bcol
            kbf = k.astype(BF16)
            low = lax.dot_general(kb.astype(BF16), kbf, _NT, preferred_element_type=F32) * decay
            tinv = _unit_lower_inverse(jnp.where(strict, low, 0.0), row, col)
            rhs = jnp.concatenate([v * bcol, kb * eg[:, la:la + 1]], axis=1)
            sol = jnp.dot(tinv, rhs, preferred_element_type=F32, precision=hi)
            value, kcum = sol[:, :dv], sol[:, dv:]
            aqk = lax.dot_general(q.astype(BF16), kbf, _NT, preferred_element_type=F32) * decay
            qg = (q * eg[:, la:la + 1]).astype(BF16)
            kg = (k * egl[:, la:la + 1]).astype(BF16)
            s = st_ref[h]
            sb = s.astype(BF16)
            v_new = value - _dot(kcum.astype(BF16), sb)
            vnb = v_new.astype(BF16)
            o = _dot(qg, sb) + _dot(aqk.astype(BF16), vnb)
            st_ref[h] = s * edl[:, la:la + 1] + lax.dot_general(kg, vnb, _TN, preferred_element_type=F32)
            o = o * lax.rsqrt(jnp.mean(o * o, axis=-1, keepdims=True) + NORM_EPS) * ng
            zc = z_ref[rows, h * dv:(h + 1) * dv].astype(F32)
            o_ref[rows, h * dv:(h + 1) * dv] = (o * _silu(zc)).astype(o_ref.dtype)
        return carry

    lax.fori_loop(0, tile // CHUNK, chunk_body, 0)


def _conv_cols(xbuf, cw_ref, cs, rows):
    acc = None
    for k in range(CONV_WIDTH):
        off = SUBLANES - (CONV_WIDTH - 1) + k
        term = cw_ref[k:k + 1, cs] * xbuf[off:off + rows, cs]
        acc = term if acc is None else acc + term
    return acc


def _dn_branch(proj, small, cols, bsz, seq, conv_w, alog_pad, dtb_pad, norm_g, heads, b_lane, a_lane, tile):
    cq = conv_w.shape[1]
    dv = norm_g.shape[1]
    dk = (cq // heads - dv) // 2
    hdv = heads * dv
    nt = seq // tile
    full = lambda b, t: (0, 0)
    qcol, zcol = cols["dn_qkv"] // cq, cols["dn_z"] // hdv
    return pl.pallas_call(
        functools.partial(_dn_kernel, tile=tile, heads=heads, dk=dk, dv=dv, b_lane=b_lane, a_lane=a_lane),
        grid=(bsz, nt),
        in_specs=[pl.BlockSpec((tile, cq), lambda b, t: (b * nt + t, qcol)),
                  pl.BlockSpec((tile, hdv), lambda b, t: (b * nt + t, zcol)),
                  pl.BlockSpec((tile, LANES), lambda b, t: (b * nt + t, 0)),
                  pl.BlockSpec((CONV_WIDTH, cq), full),
                  pl.BlockSpec((1, LANES), full),
                  pl.BlockSpec((1, LANES), full),
                  pl.BlockSpec((1, dv), full)],
        out_specs=pl.BlockSpec((tile, hdv), lambda b, t: (b * nt + t, 0)),
        out_shape=jax.ShapeDtypeStruct((bsz * seq, hdv), BF16),
        scratch_shapes=[pltpu.VMEM((tile + SUBLANES, cq), F32),
                        pltpu.VMEM((tile, cq), F32),
                        pltpu.VMEM((tile, LANES), F32),
                        pltpu.VMEM((tile, LANES), F32),
                        pltpu.VMEM((heads, dk, dv), F32)],
        compiler_params=_params("parallel", "arbitrary"),
        name="gated_deltanet",
    )(proj, proj, small, conv_w, alog_pad, dtb_pad, norm_g)


def _merge_kernel(y1_ref, y2_ref, y3_ref, gl_ref, bg_ref, wb_ref, o_ref):
    d = o_ref.shape[1]
    acc = None
    for i, y_ref in enumerate((y1_ref, y2_ref, y3_ref)):
        gate = _sigmoid(gl_ref[:, i * d:(i + 1) * d].astype(F32) + bg_ref[:, i * d:(i + 1) * d])
        term = gate * _dot(y_ref[...], wb_ref[i])
        acc = term if acc is None else acc + term
    o_ref[...] = acc.astype(o_ref.dtype)


def _merge(ys, proj, cols, b_gate, w_branch, tm):
    m, c = ys[0].shape
    nb, _, d = w_branch.shape
    gcol = cols["gate"] // (nb * d)
    yspec = pl.BlockSpec((tm, c), lambda i: (i, 0))
    return pl.pallas_call(
        _merge_kernel,
        grid=(m // tm,),
        in_specs=[yspec, yspec, yspec,
                  pl.BlockSpec((tm, nb * d), lambda i: (i, gcol)),
                  pl.BlockSpec((1, nb * d), lambda i: (0, 0)),
                  pl.BlockSpec((nb, c, d), lambda i: (0, 0, 0))],
        out_specs=pl.BlockSpec((tm, d), lambda i: (i, 0)),
        out_shape=jax.ShapeDtypeStruct((m, d), BF16),
        compiler_params=_params("parallel"),
        name="branch_merge",
    )(*ys, proj, b_gate, w_branch)


def _layernorm_store(y, g_ref, b_ref, of_ref, ob_ref):
    mu = jnp.mean(y, axis=-1, keepdims=True)
    yc = y - mu
    var = jnp.mean(yc * yc, axis=-1, keepdims=True)
    out = yc * lax.rsqrt(var + LN_EPS) * g_ref[...] + b_ref[...]
    of_ref[...] = out
    ob_ref[...] = out.astype(ob_ref.dtype)


def _outln_kernel(m_ref, w_ref, x_ref, g_ref, b_ref, of_ref, ob_ref, *, alpha):
    y = alpha * x_ref[...] + _dot(m_ref[...], w_ref[...])
    _layernorm_store(y, g_ref, b_ref, of_ref, ob_ref)


def _out_ln(merged, w_out, x, g, b, alpha, tm):
    m, d = x.shape
    row = pl.BlockSpec((tm, d), lambda i: (i, 0))
    vec = pl.BlockSpec((1, d), lambda i: (0, 0))
    return pl.pallas_call(
        functools.partial(_outln_kernel, alpha=alpha),
        grid=(m // tm,),
        in_specs=[row, pl.BlockSpec((d, d), lambda i: (0, 0)), row, vec, vec],
        out_specs=[row, row],
        out_shape=[jax.ShapeDtypeStruct((m, d), F32), jax.ShapeDtypeStruct((m, d), BF16)],
        compiler_params=_params("parallel"),
        name="out_proj_ln",
    )(merged, w_out, x, g, b)


def _mlp_kernel(xb_ref, xf_ref, w1_ref, b1_ref, w2_ref, b2_ref, g_ref, b_ref, of_ref, ob_ref, acc_ref,
                *, alpha):
    f = pl.program_id(1)
    a = jnp.maximum(_dot(xb_ref[...], w1_ref[...]) + b1_ref[...], 0.0)
    p = _dot((a * a).astype(BF16), w2_ref[...])

    @pl.when(f == 0)
    def _():
        acc_ref[...] = p

    @pl.when(f > 0)
    def _():
        acc_ref[...] += p

    @pl.when(f == pl.num_programs(1) - 1)
    def _():
        y = alpha * xf_ref[...] + (acc_ref[...] + b2_ref[...])
        _layernorm_store(y, g_ref, b_ref, of_ref, ob_ref)


def _mlp_ln(xb, xf, w1, b1, w2, b2, g, b, alpha, tm, tf):
    m, d = xf.shape
    ff = w1.shape[1]
    row = pl.BlockSpec((tm, d), lambda i, f: (i, 0))
    vec = pl.BlockSpec((1, d), lambda i, f: (0, 0))
    return pl.pallas_call(
        functools.partial(_mlp_kernel, alpha=alpha),
        grid=(m // tm, ff // tf),
        in_specs=[row, row,
                  pl.BlockSpec((d, tf), lambda i, f: (0, f)),
                  pl.BlockSpec((1, tf), lambda i, f: (0, f)),
                  pl.BlockSpec((tf, d), lambda i, f: (f, 0)),
                  vec, vec, vec],
        out_specs=[row, row],
        out_shape=[jax.ShapeDtypeStruct((m, d), F32), jax.ShapeDtypeStruct((m, d), BF16)],
        scratch_shapes=[pltpu.VMEM((tm, d), F32)],
        compiler_params=_params("parallel", "arbitrary"),
        name="mlp_ln",
    )(xb, xf, w1, b1, w2, b2, g, b)


def _column_plan(sizes):
    order = ("dn_qkv", "lru_x", "lru_y", "gla_v", "gate", "gla_r", "dn_z", "gla_q", "gla_k")
    cols, off = {}, 0
    for name in order:
        assert off % sizes[name] == 0, (name, off, sizes[name])
        cols[name] = off
        off += sizes[name]
    return order, cols, off


def _block_diag(w):
    g, n, _ = w.shape
    eye = jnp.eye(g, dtype=w.dtype)
    return (eye[:, None, :, None] * w[:, :, None, :]).reshape(g * n, g * n)


def kernel(x, w_in, lru_conv_w, lru_conv_b, lru_wa, lru_ba, lru_wi, lru_bi, lru_lambda, gla_wa2, gla_ba2, gla_norm_g, dn_conv_w, dn_a_log, dn_dt_bias, dn_norm_g, w_branch, b_gate, w_out, ln1_g, ln1_b, mlp_w1, mlp_b1, mlp_w2, mlp_b2, ln2_g, ln2_b):
    bsz, seq, d = x.shape
    depth = w_in.shape[0]
    m = bsz * seq
    alpha = (2.0 * depth) ** 0.25

    lru_w = lru_conv_w.shape[-1]
    gla_heads = 4
    gla_rank, gla_hdk = gla_wa2.shape[1], gla_wa2.shape[2]
    gla_hdv = gla_heads * gla_norm_g.shape[-1]
    dn_heads = dn_a_log.shape[-1]
    dn_cq = dn_conv_w.shape[-1]
    dn_hdv = dn_heads * dn_norm_g.shape[-1]
    nb = w_branch.shape[1]
    sizes = {"lru_x": lru_w, "lru_y": lru_w, "gla_q": gla_hdk, "gla_k": gla_hdk, "gla_v": gla_hdv,
             "gla_alr": gla_rank, "gla_r": gla_hdv, "dn_qkv": dn_cq, "dn_b": dn_heads, "dn_a": dn_heads,
             "dn_z": dn_hdv, "gate": nb * d}
    ref_order = ("lru_x", "lru_y", "gla_q", "gla_k", "gla_v", "gla_alr", "gla_r",
                 "dn_qkv", "dn_b", "dn_a", "dn_z", "gate")
    src, off = {}, 0
    for name in ref_order:
        src[name] = off
        off += sizes[name]
    assert off == w_in.shape[-1]
    order, cols, n_big = _column_plan(sizes)

    take = lambda name: w_in[:, :, src[name]:src[name] + sizes[name]]
    w_big = jnp.concatenate([take(n) for n in order], axis=-1).astype(BF16)
    b_lane, a_lane = gla_rank, gla_rank + dn_heads
    n_small = gla_rank + 2 * dn_heads
    w_small = jnp.concatenate([take("gla_alr"), take("dn_b"), take("dn_a"),
                               jnp.zeros((depth, d, LANES - n_small), w_in.dtype)], axis=-1).astype(BF16)
    wa2_pad = jnp.concatenate([gla_wa2, jnp.zeros((depth, LANES - gla_rank, gla_hdk), gla_wa2.dtype)],
                              axis=1).astype(BF16)

    def lane_pad(p, lane):
        return jnp.pad(p, ((0, 0), (lane, LANES - lane - p.shape[1])))[:, None, :]

    alog_pad = lane_pad(dn_a_log, a_lane)
    dtb_pad = lane_pad(dn_dt_bias, a_lane)
    wa_full = jax.vmap(_block_diag)(lru_wa).astype(BF16)
    wi_full = jax.vmap(_block_diag)(lru_wi).astype(BF16)
    wb16, wo16 = w_branch.astype(BF16), w_out.astype(BF16)
    w1_16, w2_16 = mlp_w1.astype(BF16), mlp_w2.astype(BF16)
    vec = lambda p, l: p[l][None, :]

    tile = min(256, seq)
    xf = x.reshape(m, d)
    xb = xf.astype(BF16)
    for l in range(depth):
        proj = _matmul(xb, w_big[l], BF16, min(1024, m), 1536 if n_big % 1536 == 0 else n_big, "in_proj")
        small = _matmul(xb, w_small[l], F32, min(1024, m), LANES, "in_proj_small")
        y_lru = _lru_branch(proj, cols, bsz, seq, lru_conv_w[l], vec(lru_conv_b, l), wa_full[l],
                            vec(lru_ba, l), wi_full[l], vec(lru_bi, l), vec(lru_lambda, l), tile, min(256, tile))
        y_gla = _gla_branch(proj, small, cols, bsz, seq, wa2_pad[l], vec(gla_ba2, l), vec(gla_norm_g, l), tile)
        y_dn = _dn_branch(proj, small, cols, bsz, seq, dn_conv_w[l], alog_pad[l], dtb_pad[l],
                          vec(dn_norm_g, l), dn_heads, b_lane, a_lane, tile)
        merged = _merge((y_lru, y_gla, y_dn), proj, cols, b_gate[l].reshape(1, nb * d), wb16[l], min(256, m))
        xf, xb = _out_ln(merged, wo16[l], xf, vec(ln1_g, l), vec(ln1_b, l), alpha, min(512, m))
        xf, xb = _mlp_ln(xb, xf, w1_16[l], vec(mlp_b1, l), w2_16[l], vec(mlp_b2, l),
                         vec(ln2_g, l), vec(ln2_b, l), alpha, min(512, m), 1024)
    return xf.reshape(bsz, seq, d)
```

```python
import functools

import jax
import jax.numpy as jnp
from jax import lax
from jax.experimental import pallas as pl
from jax.experimental.pallas import tpu as pltpu

F32 = jnp.float32
BF16 = jnp.bfloat16

CONV_WIDTH = 4
CHUNK = 64
LRU_C = 8.0
GLA_TAU = 16.0
LN_EPS = 1e-5
NORM_EPS = 1e-6

SUBLANES = 8
LANES = 128
VMEM_LIMIT_BYTES = 56 * 1024 * 1024

_NT = (((1,), (1,)), ((), ()))
_TN = (((0,), (0,)), ((), ()))


def _params(*sem):
    return pltpu.CompilerParams(dimension_semantics=sem, vmem_limit_bytes=VMEM_LIMIT_BYTES)


def _sigmoid(x):
    return 0.5 * (1.0 + jnp.tanh(0.5 * x))


def _softplus(x):
    return jnp.maximum(x, 0.0) + jnp.log1p(jnp.exp(-jnp.abs(x)))


def _silu(x):
    return x * _sigmoid(x)


def _gelu_tanh(x):
    return x * (0.5 * (1.0 + jnp.tanh(0.7978845608028654 * (x + 0.044715 * (x * x * x)))))


def _dot(a, b):
    return jnp.dot(a, b, preferred_element_type=F32)


def _group_cumsum(x, period):
    pos = lax.broadcasted_iota(jnp.int32, x.shape, 0) & (period - 1)
    k = 1
    while k < period:
        x = x + jnp.where(pos >= k, pltpu.roll(x, k, 0), 0.0)
        k *= 2
    return x


def _mm_kernel(x_ref, w_ref, o_ref):
    o_ref[...] = _dot(x_ref[...], w_ref[...]).astype(o_ref.dtype)


def _matmul(x, w, out_dtype, tm, tn, name):
    m, k = x.shape
    n = w.shape[1]
    assert m % tm == 0 and n % tn == 0
    return pl.pallas_call(
        _mm_kernel,
        grid=(n // tn, m // tm),
        in_specs=[pl.BlockSpec((tm, k), lambda j, i: (i, 0)),
                  pl.BlockSpec((k, tn), lambda j, i: (0, j))],
        out_specs=pl.BlockSpec((tm, tn), lambda j, i: (i, j)),
        out_shape=jax.ShapeDtypeStruct((m, n), out_dtype),
        compiler_params=_params("parallel", "parallel"),
        name=name,
    )(x, w)


def _load_conv_tile(x_ref, xbuf, t, tile):
    c = xbuf.shape[1]

    @pl.when(t == 0)
    def _():
        xbuf[0:SUBLANES, :] = jnp.zeros((SUBLANES, c), F32)

    @pl.when(t > 0)
    def _():
        xbuf[0:SUBLANES, :] = xbuf[tile:tile + SUBLANES, :]

    xbuf[SUBLANES:SUBLANES + tile, :] = x_ref[...].astype(F32)


def _conv_rows(xbuf, cw_ref, start, rows):
    acc = None
    for k in range(CONV_WIDTH):
        off = SUBLANES - (CONV_WIDTH - 1) + k + start
        term = cw_ref[k:k + 1, :] * xbuf[off:off + rows, :]
        acc = term if acc is None else acc + term
    return acc


def _lru_kernel(x_ref, y_ref, cw_ref, cb_ref, wa_ref, ba_ref, wi_ref, bi_ref, lam_ref,
                o_ref, xbuf, hcar, *, tile, sub):
    t = pl.program_id(1)
    c = xbuf.shape[1]
    _load_conv_tile(x_ref, xbuf, t, tile)

    @pl.when(t == 0)
    def _():
        hcar[...] = jnp.zeros(hcar.shape, F32)

    neg_c_sp = -LRU_C * _softplus(-lam_ref[...])
    pos = lax.broadcasted_iota(jnp.int32, (sub, c), 0) & (SUBLANES - 1)
    h = hcar[0:1, :]
    for s in range(tile // sub):
        xc = _conv_rows(xbuf, cw_ref, s * sub, sub) + cb_ref[...]
        xcb = xc.astype(BF16)
        r = _sigmoid(_dot(xcb, wa_ref[...]) + ba_ref[...])
        i = _sigmoid(_dot(xcb, wi_ref[...]) + bi_ref[...])
        log_a = neg_c_sp * r
        a = jnp.exp(log_a)
        u = jnp.sqrt(-jnp.tanh(log_a) * (a * a + 1.0)) * (i * xc)
        k = 1
        while k < SUBLANES:
            m = pos >= k
            u = jnp.where(m, a * pltpu.roll(u, k, 0) + u, u)
            a = jnp.where(m, a * pltpu.roll(a, k, 0), a)
            k *= 2
        outs = []
        for g in range(sub // SUBLANES):
            hg = a[g * SUBLANES:(g + 1) * SUBLANES] * h + u[g * SUBLANES:(g + 1) * SUBLANES]
            h = hg[SUBLANES - 1:SUBLANES]
            outs.append(hg)
        hs = jnp.concatenate(outs, axis=0)
        yv = y_ref[s * sub:(s + 1) * sub, :].astype(F32)
        o_ref[s * sub:(s + 1) * sub, :] = (hs * _gelu_tanh(yv)).astype(o_ref.dtype)
    hcar[...] = jnp.broadcast_to(h, hcar.shape)


def _lru_branch(proj, cols, bsz, seq, conv_w, conv_b, wa_full, ba, wi_full, bi, lam, tile, sub):
    c = conv_w.shape[1]
    nt = seq // tile
    xcol, ycol = cols["lru_x"] // c, cols["lru_y"] // c
    full = lambda b, t: (0, 0)
    return pl.pallas_call(
        functools.partial(_lru_kernel, tile=tile, sub=sub),
        grid=(bsz, nt),
        in_specs=[pl.BlockSpec((tile, c), lambda b, t: (b * nt + t, xcol)),
                  pl.BlockSpec((tile, c), lambda b, t: (b * nt + t, ycol)),
                  pl.BlockSpec((CONV_WIDTH, c), full),
                  pl.BlockSpec((1, c), full),
                  pl.BlockSpec((c, c), full),
                  pl.BlockSpec((1, c), full),
                  pl.BlockSpec((c, c), full),
                  pl.BlockSpec((1, c), full),
                  pl.BlockSpec((1, c), full)],
        out_specs=pl.BlockSpec((tile, c), lambda b, t: (b * nt + t, 0)),
        out_shape=jax.ShapeDtypeStruct((bsz * seq, c), BF16),
        scratch_shapes=[pltpu.VMEM((tile + SUBLANES, c), F32), pltpu.VMEM((SUBLANES, c), F32)],
        compiler_params=_params("parallel", "arbitrary"),
        name="rg_lru",
    )(proj, proj, conv_w, conv_b, wa_full, ba, wi_full, bi, lam)


def _gla_kernel(q_ref, k_ref, v_ref, r_ref, sm_ref, wa2_ref, ba2_ref, ng_ref, o_ref, st_ref,
                *, tile, heads, dk, dv):
    t = pl.program_id(1)

    @pl.when(t == 0)
    def _():
        st_ref[...] = jnp.zeros(st_ref.shape, F32)

    z = _dot(sm_ref[...].astype(BF16), wa2_ref[...]) + ba2_ref[...]
    gk = (jnp.minimum(z, 0.0) - jnp.log1p(jnp.exp(-jnp.abs(z)))) / GLA_TAU
    b = _group_cumsum(gk, CHUNK)
    row = lax.broadcasted_iota(jnp.int32, (CHUNK, CHUNK), 0)
    col = lax.broadcasted_iota(jnp.int32, (CHUNK, CHUNK), 1)
    causal = row >= col
    scale = dk ** -0.5
    ng = ng_ref[...]
    for c in range(tile // CHUNK):
        rows = slice(c * CHUNK, (c + 1) * CHUNK)
        bc = b[rows]
        bl = bc[CHUNK - 1:CHUNK]
        qf = q_ref[rows, :].astype(F32)
        kf = k_ref[rows, :].astype(F32)
        qe = ((qf * scale) * jnp.exp(bc)).astype(BF16)
        ke = (kf * jnp.exp(-bc)).astype(BF16)
        kd = (kf * jnp.exp(bl - bc)).astype(BF16)
        dec = jnp.exp(bl)
        for h in range(heads):
            ks = slice(h * dk, (h + 1) * dk)
            vs = slice(h * dv, (h + 1) * dv)
            vh = v_ref[rows, vs]
            att = lax.dot_general(qe[:, ks], ke[:, ks], _NT, preferred_element_type=F32)
            att = jnp.where(causal, att, 0.0)
            st = st_ref[h]
            o = _dot(att.astype(BF16), vh) + lax.dot_general(
                qe[:, ks], st.astype(BF16), _NT, preferred_element_type=F32)
            st_ref[h] = st * dec[:, ks] + lax.dot_general(
                vh, kd[:, ks], _TN, preferred_element_type=F32)
            o = o * lax.rsqrt(jnp.mean(o * o, axis=-1, keepdims=True) + NORM_EPS) * ng
            o_ref[rows, vs] = (o * _silu(r_ref[rows, vs].astype(F32))).astype(o_ref.dtype)


def _gla_branch(proj, small, cols, bsz, seq, wa2_pad, ba2, norm_g, tile):
    hdk = wa2_pad.shape[1]
    dv = norm_g.shape[1]
    heads = 4
    dk = hdk // heads
    hdv = heads * dv
    nt = seq // tile
    full = lambda b, t: (0, 0)
    qcol, kcol = cols["gla_q"] // hdk, cols["gla_k"] // hdk
    vcol, rcol = cols["gla_v"] // hdv, cols["gla_r"] // hdv
    return pl.pallas_call(
        functools.partial(_gla_kernel, tile=tile, heads=heads, dk=dk, dv=dv),
        grid=(bsz, nt),
        in_specs=[pl.BlockSpec((tile, hdk), lambda b, t: (b * nt + t, qcol)),
                  pl.BlockSpec((tile, hdk), lambda b, t: (b * nt + t, kcol)),
                  pl.BlockSpec((tile, hdv), lambda b, t: (b * nt + t, vcol)),
                  pl.BlockSpec((tile, hdv), lambda b, t: (b * nt + t, rcol)),
                  pl.BlockSpec((tile, LANES), lambda b, t: (b * nt + t, 0)),
                  pl.BlockSpec((LANES, hdk), full),
                  pl.BlockSpec((1, hdk), full),
                  pl.BlockSpec((1, dv), full)],
        out_specs=pl.BlockSpec((tile, hdv), lambda b, t: (b * nt + t, 0)),
        out_shape=jax.ShapeDtypeStruct((bsz * seq, hdv), BF16),
        scratch_shapes=[pltpu.VMEM((heads, dv, dk), F32)],
        compiler_params=_params("parallel", "arbitrary"),
        name="gla",
    )(proj, proj, proj, proj, small, wa2_pad, ba2, norm_g)


def _pair_blockdiag(y, lo):
    return jnp.concatenate([jnp.where(lo, y, 0.0), jnp.where(lo, 0.0, y)], axis=0).astype(BF16)


def _unit_lower_inverse_minus_eye(mats, row, col, lo):
    def mul(xs, ys):
        return [jnp.dot(x.astype(BF16), _pair_blockdiag(y, lo), preferred_element_type=F32)
                for x, y in zip(xs, ys)]

    eye = (row == col).astype(F32)
    blk16 = (row // 16) == (col // 16)
    blk32 = (row // 32) == (col // 32)
    off16 = jnp.logical_and(blk32, jnp.logical_not(blk16))
    d = [jnp.where(blk16, a, 0.0) for a in mats]
    d2 = mul(d, d)
    d4 = mul(d2, d2)
    d8 = mul(d4, d4)
    t = mul([eye - x for x in d], [eye + x for x in d2])
    t = mul(t, [eye + x for x in d4])
    t = mul(t, [eye + x for x in d8])
    c1 = [jnp.where(off16, a, 0.0) for a in mats]
    u = mul(mul(t, c1), t)
    t = [x - y for x, y in zip(t, u)]
    c2 = [jnp.where(blk32, 0.0, a) for a in mats]
    u = mul(mul(t, c2), t)
    return [x - y - eye for x, y in zip(t, u)]


def _dn_kernel(qkv_ref, z_ref, sm_ref, cw_ref, alog_ref, dtb_ref, ng_ref, o_ref,
               xbuf, act, gsc, bsc, val_s, kc_s, qg_s, kg_s, aq_s, st_ref,
               *, tile, heads, dk, dv, b_lane, a_lane, group):
    t = pl.program_id(1)
    _load_conv_tile(qkv_ref, xbuf, t, tile)

    @pl.when(t == 0)
    def _():
        st_ref[...] = jnp.zeros(st_ref.shape, F32)

    hk = heads * dk
    for h in range(heads):
        for part, scale in ((0, dk ** -0.5), (1, 1.0)):
            cs = slice(part * hk + h * dk, part * hk + (h + 1) * dk)
            x = _silu(_conv_cols(xbuf, cw_ref, cs, tile))
            x = x * lax.rsqrt(jnp.sum(x * x, axis=-1, keepdims=True) + NORM_EPS)
            act[:, cs] = x * scale if part == 0 else x
    vs_all = slice(2 * hk, 2 * hk + heads * dv)
    act[:, vs_all] = _silu(_conv_cols(xbuf, cw_ref, vs_all, tile))

    sm = sm_ref[...]
    bsc[...] = _sigmoid(sm)
    gsc[...] = _group_cumsum(-jnp.exp(alog_ref[...]) * _softplus(sm + dtb_ref[...]), CHUNK)

    row = lax.broadcasted_iota(jnp.int32, (CHUNK, LANES), 0)
    lane = lax.broadcasted_iota(jnp.int32, (CHUNK, LANES), 1)
    col = lane & (CHUNK - 1)
    lo = lane < CHUNK
    lo_row = lo[0:1]
    incl = row >= col
    strict = row > col
    first = lax.broadcasted_iota(jnp.int32, (CHUNK, 2 * dk), 1) < dk
    ng = ng_ref[...]
    pairs = heads // 2

    def pick(arr, l0, wide):
        return jnp.where(first if wide else lo, arr[:, l0:l0 + 1], arr[:, l0 + 1:l0 + 2])

    def prepare(gidx, carry):
        low, rhs2 = [], []
        for cc in range(group):
            r0 = pl.multiple_of((gidx * group + cc) * CHUNK, CHUNK)
            rows = pl.ds(r0, CHUNK)
            gc = gsc[rows, :]
            be = bsc[rows, :]
            eg = jnp.exp(gc)
            egl = jnp.exp(gc[CHUNK - 1:CHUNK, :] - gc)
            gct = jnp.concatenate([gc, gc], axis=0).T
            for p in range(pairs):
                h0 = 2 * p
                q2 = act[rows, h0 * dk:(h0 + 2) * dk]
                k2 = act[rows, hk + h0 * dk:hk + (h0 + 2) * dk]
                v2 = act[rows, 2 * hk + h0 * dv:2 * hk + (h0 + 2) * dv]
                bcol = pick(be, b_lane + h0, True)
                egc = pick(eg, a_lane + h0, True)
                kb2 = k2 * bcol
                lhs = jnp.concatenate([kb2, q2], axis=0).astype(BF16)
                kbd = jnp.concatenate([jnp.where(first, k2, 0.0), jnp.where(first, 0.0, k2)],
                                      axis=0).astype(BF16)
                both = lax.dot_general(lhs, kbd, _NT, preferred_element_type=F32)
                la = a_lane + h0
                diff = pick(gc, la, False) - jnp.where(lo_row, gct[la:la + 1, :], gct[la + 1:la + 2, :])
                decay = jnp.where(incl, jnp.exp(jnp.where(incl, diff, 0.0)), 0.0)
                low.append(jnp.where(strict, both[:CHUNK] * decay, 0.0))
                aqk = both[CHUNK:] * decay
                aq_s[rows, p * 2 * LANES:(p * 2 + 1) * LANES] = jnp.where(lo, aqk, 0.0).astype(BF16)
                aq_s[rows, (p * 2 + 1) * LANES:(p * 2 + 2) * LANES] = jnp.where(lo, 0.0, aqk).astype(BF16)
                vb, kbe = v2 * bcol, kb2 * egc
                rhs2.append((rows, h0, jnp.concatenate(
                    [jnp.concatenate([vb[:, :dv], kbe[:, :dk]], axis=1),
                     jnp.concatenate([vb[:, dv:], kbe[:, dk:]], axis=1)], axis=0)))
                qg_s[rows, h0 * dk:(h0 + 2) * dk] = (q2 * egc).astype(BF16)
                kg_s[rows, h0 * dk:(h0 + 2) * dk] = (k2 * pick(egl, a_lane + h0, True)).astype(BF16)
        nmat = _unit_lower_inverse_minus_eye(low, row, col, lo)
        for n, (rows, h0, rhs) in zip(nmat, rhs2):
            sol = rhs + _dot(_pair_blockdiag(n, lo), rhs.astype(BF16))
            for j in range(2):
                h = h0 + j
                val_s[rows, h * dv:(h + 1) * dv] = sol[j * CHUNK:(j + 1) * CHUNK, :dv]
                kc_s[rows, h * dk:(h + 1) * dk] = sol[j * CHUNK:(j + 1) * CHUNK, dv:].astype(BF16)
        return carry

    lax.fori_loop(0, tile // (CHUNK * group), prepare, 0)

    def recur(c, carry):
        r0 = pl.multiple_of(c * CHUNK, CHUNK)
        rows = pl.ds(r0, CHUNK)
        edl = jnp.exp(gsc[rows, :][CHUNK - 1:CHUNK, :])
        states = [st_ref[h] for h in range(heads)]
        both = [_dot(jnp.concatenate([kc_s[rows, h * dk:(h + 1) * dk], qg_s[rows, h * dk:(h + 1) * dk]], axis=0),
                     states[h].astype(BF16)) for h in range(heads)]
        vnb = [(val_s[rows, h * dv:(h + 1) * dv] - both[h][:CHUNK]).astype(BF16) for h in range(heads)]
        intra = [_dot(jnp.concatenate([aq_s[rows, p * 2 * LANES:(p * 2 + 1) * LANES],
                                       aq_s[rows, (p * 2 + 1) * LANES:(p * 2 + 2) * LANES]], axis=0),
                      jnp.concatenate([vnb[2 * p], vnb[2 * p + 1]], axis=0)) for p in range(pairs)]
        for h in range(heads):
            la = a_lane + h
            st_ref[h] = states[h] * edl[:, la:la + 1] + lax.dot_general(
                kg_s[rows, h * dk:(h + 1) * dk], vnb[h], _TN, preferred_element_type=F32)
        for h in range(heads):
            j = h % 2
            o = both[h][CHUNK:] + intra[h // 2][j * CHUNK:(j + 1) * CHUNK]
            o = o * lax.rsqrt(jnp.mean(o * o, axis=-1, keepdims=True) + NORM_EPS) * ng
            zc = z_ref[rows, h * dv:(h + 1) * dv].astype(F32)
            o_ref[rows, h * dv:(h + 1) * dv] = (o * _silu(zc)).astype(o_ref.dtype)
        return carry

    lax.fori_loop(0, tile // CHUNK, recur, 0)


def _conv_cols(xbuf, cw_ref, cs, rows):
    acc = None
    for k in range(CONV_WIDTH):
        off = SUBLANES - (CONV_WIDTH - 1) + k
        term = cw_ref[k:k + 1, cs] * xbuf[off:off + rows, cs]
        acc = term if acc is None else acc + term
    return acc


def _dn_branch(proj, small, cols, bsz, seq, conv_w, alog_pad, dtb_pad, norm_g, heads, b_lane, a_lane, tile):
    cq = conv_w.shape[1]
    dv = norm_g.shape[1]
    dk = (cq // heads - dv) // 2
    assert dk == dv == LANES and 2 * CHUNK == LANES and heads % 2 == 0
    hdv = heads * dv
    nt = seq // tile
    full = lambda b, t: (0, 0)
    qcol, zcol = cols["dn_qkv"] // cq, cols["dn_z"] // hdv
    group = 2 if tile % (2 * CHUNK) == 0 else 1
    return pl.pallas_call(
        functools.partial(_dn_kernel, tile=tile, heads=heads, dk=dk, dv=dv, b_lane=b_lane, a_lane=a_lane,
                          group=group),
        grid=(bsz, nt),
        in_specs=[pl.BlockSpec((tile, cq), lambda b, t: (b * nt + t, qcol)),
                  pl.BlockSpec((tile, hdv), lambda b, t: (b * nt + t, zcol)),
                  pl.BlockSpec((tile, LANES), lambda b, t: (b * nt + t, 0)),
                  pl.BlockSpec((CONV_WIDTH, cq), full),
                  pl.BlockSpec((1, LANES), full),
                  pl.BlockSpec((1, LANES), full),
                  pl.BlockSpec((1, dv), full)],
        out_specs=pl.BlockSpec((tile, hdv), lambda b, t: (b * nt + t, 0)),
        out_shape=jax.ShapeDtypeStruct((bsz * seq, hdv), BF16),
        scratch_shapes=[pltpu.VMEM((tile + SUBLANES, cq), F32),
                        pltpu.VMEM((tile, cq), F32),
                        pltpu.VMEM((tile, LANES), F32),
                        pltpu.VMEM((tile, LANES), F32),
                        pltpu.VMEM((tile, hdv), F32),
                        pltpu.VMEM((tile, heads * dk), BF16),
                        pltpu.VMEM((tile, heads * dk), BF16),
                        pltpu.VMEM((tile, heads * dk), BF16),
                        pltpu.VMEM((tile, heads * LANES), BF16),
                        pltpu.VMEM((heads, dk, dv), F32)],
        compiler_params=_params("parallel", "arbitrary"),
        name="gated_deltanet",
    )(proj, proj, small, conv_w, alog_pad, dtb_pad, norm_g)


def _merge_kernel(y1_ref, y2_ref, y3_ref, gl_ref, bg_ref, wb_ref, o_ref):
    d = o_ref.shape[1]
    acc = None
    for i, y_ref in enumerate((y1_ref, y2_ref, y3_ref)):
        gate = _sigmoid(gl_ref[:, i * d:(i + 1) * d].astype(F32) + bg_ref[:, i * d:(i + 1) * d])
        term = gate * _dot(y_ref[...], wb_ref[i])
        acc = term if acc is None else acc + term
    o_ref[...] = acc.astype(o_ref.dtype)


def _merge(ys, proj, cols, b_gate, w_branch, tm):
    m, c = ys[0].shape
    nb, _, d = w_branch.shape
    gcol = cols["gate"] // (nb * d)
    yspec = pl.BlockSpec((tm, c), lambda i: (i, 0))
    return pl.pallas_call(
        _merge_kernel,
        grid=(m // tm,),
        in_specs=[yspec, yspec, yspec,
                  pl.BlockSpec((tm, nb * d), lambda i: (i, gcol)),
                  pl.BlockSpec((1, nb * d), lambda i: (0, 0)),
                  pl.BlockSpec((nb, c, d), lambda i: (0, 0, 0))],
        out_specs=pl.BlockSpec((tm, d), lambda i: (i, 0)),
        out_shape=jax.ShapeDtypeStruct((m, d), BF16),
        compiler_params=_params("parallel"),
        name="branch_merge",
    )(*ys, proj, b_gate, w_branch)


def _layernorm_store(y, g_ref, b_ref, of_ref, ob_ref):
    mu = jnp.mean(y, axis=-1, keepdims=True)
    yc = y - mu
    var = jnp.mean(yc * yc, axis=-1, keepdims=True)
    out = yc * lax.rsqrt(var + LN_EPS) * g_ref[...] + b_ref[...]
    of_ref[...] = out
    ob_ref[...] = out.astype(ob_ref.dtype)


def _outln_kernel(m_ref, w_ref, x_ref, g_ref, b_ref, of_ref, ob_ref, *, alpha):
    y = alpha * x_ref[...] + _dot(m_ref[...], w_ref[...])
    _layernorm_store(y, g_ref, b_ref, of_ref, ob_ref)


def _out_ln(merged, w_out, x, g, b, alpha, tm):
    m, d = x.shape
    row = pl.BlockSpec((tm, d), lambda i: (i, 0))
    vec = pl.BlockSpec((1, d), lambda i: (0, 0))
    return pl.pallas_call(
        functools.partial(_outln_kernel, alpha=alpha),
        grid=(m // tm,),
        in_specs=[row, pl.BlockSpec((d, d), lambda i: (0, 0)), row, vec, vec],
        out_specs=[row, row],
        out_shape=[jax.ShapeDtypeStruct((m, d), F32), jax.ShapeDtypeStruct((m, d), BF16)],
        compiler_params=_params("parallel"),
        name="out_proj_ln",
    )(merged, w_out, x, g, b)


def _mlp_kernel(xb_ref, xf_ref, w1_ref, b1_ref, w2_ref, b2_ref, g_ref, b_ref, of_ref, ob_ref, acc_ref,
                *, alpha):
    f = pl.program_id(1)
    a = jnp.maximum(_dot(xb_ref[...], w1_ref[...]) + b1_ref[...], 0.0)
    p = _dot((a * a).astype(BF16), w2_ref[...])

    @pl.when(f == 0)
    def _():
        acc_ref[...] = p

    @pl.when(f > 0)
    def _():
        acc_ref[...] += p

    @pl.when(f == pl.num_programs(1) - 1)
    def _():
        y = alpha * xf_ref[...] + (acc_ref[...] + b2_ref[...])
        _layernorm_store(y, g_ref, b_ref, of_ref, ob_ref)


def _mlp_ln(xb, xf, w1, b1, w2, b2, g, b, alpha, tm, tf):
    m, d = xf.shape
    ff = w1.shape[1]
    row = pl.BlockSpec((tm, d), lambda i, f: (i, 0))
    vec = pl.BlockSpec((1, d), lambda i, f: (0, 0))
    return pl.pallas_call(
        functools.partial(_mlp_kernel, alpha=alpha),
        grid=(m // tm, ff // tf),
        in_specs=[row, row,
                  pl.BlockSpec((d, tf), lambda i, f: (0, f)),
                  pl.BlockSpec((1, tf), lambda i, f: (0, f)),
                  pl.BlockSpec((tf, d), lambda i, f: (f, 0)),
                  vec, vec, vec],
        out_specs=[row, row],
        out_shape=[jax.ShapeDtypeStruct((m, d), F32), jax.ShapeDtypeStruct((m, d), BF16)],
        scratch_shapes=[pltpu.VMEM((tm, d), F32)],
        compiler_params=_params("parallel", "arbitrary"),
        name="mlp_ln",
    )(xb, xf, w1, b1, w2, b2, g, b)


def _column_plan(sizes):
    order = ("dn_qkv", "lru_x", "lru_y", "gla_v", "gate", "gla_r", "dn_z", "gla_q", "gla_k")
    cols, off = {}, 0
    for name in order:
        assert off % sizes[name] == 0, (name, off, sizes[name])
        cols[name] = off
        off += sizes[name]
    return order, cols, off


def _block_diag(w):
    g, n, _ = w.shape
    eye = jnp.eye(g, dtype=w.dtype)
    return (eye[:, None, :, None] * w[:, :, None, :]).reshape(g * n, g * n)


def kernel(x, w_in, lru_conv_w, lru_conv_b, lru_wa, lru_ba, lru_wi, lru_bi, lru_lambda, gla_wa2, gla_ba2, gla_norm_g, dn_conv_w, dn_a_log, dn_dt_bias, dn_norm_g, w_branch, b_gate, w_out, ln1_g, ln1_b, mlp_w1, mlp_b1, mlp_w2, mlp_b2, ln2_g, ln2_b):
    bsz, seq, d = x.shape
    depth = w_in.shape[0]
    m = bsz * seq
    alpha = (2.0 * depth) ** 0.25

    lru_w = lru_conv_w.shape[-1]
    gla_heads = 4
    gla_rank, gla_hdk = gla_wa2.shape[1], gla_wa2.shape[2]
    gla_hdv = gla_heads * gla_norm_g.shape[-1]
    dn_heads = dn_a_log.shape[-1]
    dn_cq = dn_conv_w.shape[-1]
    dn_hdv = dn_heads * dn_norm_g.shape[-1]
    nb = w_branch.shape[1]
    sizes = {"lru_x": lru_w, "lru_y": lru_w, "gla_q": gla_hdk, "gla_k": gla_hdk, "gla_v": gla_hdv,
             "gla_alr": gla_rank, "gla_r": gla_hdv, "dn_qkv": dn_cq, "dn_b": dn_heads, "dn_a": dn_heads,
             "dn_z": dn_hdv, "gate": nb * d}
    ref_order = ("lru_x", "lru_y", "gla_q", "gla_k", "gla_v", "gla_alr", "gla_r",
                 "dn_qkv", "dn_b", "dn_a", "dn_z", "gate")
    src, off = {}, 0
    for name in ref_order:
        src[name] = off
        off += sizes[name]
    assert off == w_in.shape[-1]
    order, cols, n_big = _column_plan(sizes)

    take = lambda name: w_in[:, :, src[name]:src[name] + sizes[name]]
    w_big = jnp.concatenate([take(n) for n in order], axis=-1).astype(BF16)
    b_lane, a_lane = gla_rank, gla_rank + dn_heads
    n_small = gla_rank + 2 * dn_heads
    w_small = jnp.concatenate([take("gla_alr"), take("dn_b"), take("dn_a"),
                               jnp.zeros((depth, d, LANES - n_small), w_in.dtype)], axis=-1).astype(BF16)
    wa2_pad = jnp.concatenate([gla_wa2, jnp.zeros((depth, LANES - gla_rank, gla_hdk), gla_wa2.dtype)],
                              axis=1).astype(BF16)

    def lane_pad(p, lane):
        return jnp.pad(p, ((0, 0), (lane, LANES - lane - p.shape[1])))[:, None, :]

    alog_pad = lane_pad(dn_a_log, a_lane)
    dtb_pad = lane_pad(dn_dt_bias, a_lane)
    wa_full = jax.vmap(_block_diag)(lru_wa).astype(BF16)
    wi_full = jax.vmap(_block_diag)(lru_wi).astype(BF16)
    wb16, wo16 = w_branch.astype(BF16), w_out.astype(BF16)
    w1_16, w2_16 = mlp_w1.astype(BF16), mlp_w2.astype(BF16)
    vec = lambda p, l: p[l][None, :]

    tile = min(256, seq)
    xf = x.reshape(m, d)
    xb = xf.astype(BF16)
    for l in range(depth):
        proj = _matmul(xb, w_big[l], BF16, min(1024, m), 1536 if n_big % 1536 == 0 else n_big, "in_proj")
        small = _matmul(xb, w_small[l], F32, min(1024, m), LANES, "in_proj_small")
        y_lru = _lru_branch(proj, cols, bsz, seq, lru_conv_w[l], vec(lru_conv_b, l), wa_full[l],
                            vec(lru_ba, l), wi_full[l], vec(lru_bi, l), vec(lru_lambda, l), tile, min(256, tile))
        y_gla = _gla_branch(proj, small, cols, bsz, seq, wa2_pad[l], vec(gla_ba2, l), vec(gla_norm_g, l), tile)
        y_dn = _dn_branch(proj, small, cols, bsz, seq, dn_conv_w[l], alog_pad[l], dtb_pad[l],
                          vec(dn_norm_g, l), dn_heads, b_lane, a_lane, tile)
        merged = _merge((y_lru, y_gla, y_dn), proj, cols, b_gate[l].reshape(1, nb * d), wb16[l], min(256, m))
        xf, xb = _out_ln(merged, wo16[l], xf, vec(ln1_g, l), vec(ln1_b, l), alpha, min(512, m))
        xf, xb = _mlp_ln(xb, xf, w1_16[l], vec(mlp_b1, l), w2_16[l], vec(mlp_b2, l),
                         vec(ln2_g, l), vec(ln2_b, l), alpha, min(512, m), 1024)
    return xf.reshape(bsz, seq, d)
```

```python
import functools

import jax
import jax.numpy as jnp
from jax import lax
from jax.experimental import pallas as pl
from jax.experimental.pallas import tpu as pltpu

F32 = jnp.float32
BF16 = jnp.bfloat16

CONV_WIDTH = 4
CHUNK = 64
LRU_C = 8.0
GLA_TAU = 16.0
LN_EPS = 1e-5
NORM_EPS = 1e-6

SUBLANES = 8
LANES = 128
VMEM_LIMIT_BYTES = 56 * 1024 * 1024
ROW_BLOCKS = 2

_NT = (((1,), (1,)), ((), ()))
_TN = (((0,), (0,)), ((), ()))


def _params(*sem):
    return pltpu.CompilerParams(dimension_semantics=sem, vmem_limit_bytes=VMEM_LIMIT_BYTES)


def _sigmoid(x):
    return 0.5 * (1.0 + jnp.tanh(0.5 * x))


def _softplus(x):
    return jnp.maximum(x, 0.0) + jnp.log1p(jnp.exp(-jnp.abs(x)))


def _silu(x):
    h = 0.5 * x
    return h + h * jnp.tanh(h)


def _gelu_tanh(x):
    return x * (0.5 * (1.0 + jnp.tanh(0.7978845608028654 * (x + 0.044715 * (x * x * x)))))


def _dot(a, b):
    return jnp.dot(a, b, preferred_element_type=F32)


def _group_cumsum(x, period):
    pos = lax.broadcasted_iota(jnp.int32, x.shape, 0) & (period - 1)
    k = 1
    while k < period:
        x = x + jnp.where(pos >= k, pltpu.roll(x, k, 0), 0.0)
        k *= 2
    return x


def _mm_kernel(x_ref, w_ref, o_ref):
    o_ref[...] = _dot(x_ref[...], w_ref[...]).astype(o_ref.dtype)


def _matmul(x, w, out_dtype, tm, tn, name):
    m, k = x.shape
    n = w.shape[1]
    assert m % tm == 0 and n % tn == 0
    return pl.pallas_call(
        _mm_kernel,
        grid=(n // tn, m // tm),
        in_specs=[pl.BlockSpec((tm, k), lambda j, i: (i, 0)),
                  pl.BlockSpec((k, tn), lambda j, i: (0, j))],
        out_specs=pl.BlockSpec((tm, tn), lambda j, i: (i, j)),
        out_shape=jax.ShapeDtypeStruct((m, n), out_dtype),
        compiler_params=_params("parallel", "parallel"),
        name=name,
    )(x, w)


def _load_conv_tile(x_ref, xbuf, t, tile):
    c = xbuf.shape[1]

    @pl.when(t == 0)
    def _():
        xbuf[0:SUBLANES, :] = jnp.zeros((SUBLANES, c), F32)

    @pl.when(t > 0)
    def _():
        xbuf[0:SUBLANES, :] = xbuf[tile:tile + SUBLANES, :]

    xbuf[SUBLANES:SUBLANES + tile, :] = x_ref[...].astype(F32)


def _conv_rows(xbuf, cw_ref, start, rows):
    acc = None
    for k in range(CONV_WIDTH):
        off = SUBLANES - (CONV_WIDTH - 1) + k + start
        term = cw_ref[k:k + 1, :] * xbuf[off:off + rows, :]
        acc = term if acc is None else acc + term
    return acc


UNIT = SUBLANES * SUBLANES


def _lru_kernel(x_ref, y_ref, cw_ref, cb_ref, wa_ref, ba_ref, wi_ref, bi_ref, lam_ref,
                o_ref, xs, hs, hcar, *, tile):
    t = pl.program_id(1)
    slabs = xs.shape[0]
    c = slabs * LANES
    groups = wa_ref.shape[0]
    gw = c // groups

    @pl.when(t == 0)
    def _():
        hcar[...] = jnp.zeros(hcar.shape, F32)
        for s in range(slabs):
            xs[s, 0:SUBLANES, :] = jnp.zeros((SUBLANES, LANES), F32)

    @pl.when(t > 0)
    def _():
        for s in range(slabs):
            xs[s, 0:SUBLANES, :] = xs[s, tile:tile + SUBLANES, :]

    for s in range(slabs):
        xs[s, SUBLANES:SUBLANES + tile, :] = x_ref[:, s * LANES:(s + 1) * LANES].astype(F32)

    units = tile // UNIT
    taps = CONV_WIDTH - 1
    xc_units = []
    for u in range(units):
        per_slab = []
        for s in range(slabs):
            ls = slice(s * LANES, (s + 1) * LANES)
            x = {j: xs[s, pl.ds(SUBLANES + u * UNIT + j, SUBLANES, stride=SUBLANES), :]
                 for j in range(-taps, SUBLANES)}
            regs = []
            for j in range(SUBLANES):
                acc = cb_ref[:, ls] + cw_ref[taps:taps + 1, ls] * x[j]
                for k in range(taps):
                    acc = acc + cw_ref[k:k + 1, ls] * x[j - taps + k]
                regs.append(acc)
            per_slab.append(jnp.concatenate(regs, axis=0))
        xc_units.append(jnp.concatenate(per_slab, axis=1))
    xc = jnp.concatenate(xc_units, axis=0)
    xcb = xc.astype(BF16)

    def gate(w_ref, b_ref):
        pre = jnp.concatenate([_dot(xcb[:, g * gw:(g + 1) * gw], w_ref[g]) for g in range(groups)], axis=1)
        return _sigmoid(pre + b_ref[...])

    r = gate(wa_ref, ba_ref)
    i = gate(wi_ref, bi_ref)
    log_a = (-LRU_C * _softplus(-lam_ref[...])) * r
    a = jnp.exp(log_a)
    v = jnp.sqrt(-jnp.tanh(log_a) * (a * a + 1.0)) * (i * xc)

    q = lax.broadcasted_iota(jnp.int32, (SUBLANES, c), 0)
    cin = hcar[0:1, :]
    for u in range(units):
        reg = lambda arr, j: arr[u * UNIT + j * SUBLANES:u * UNIT + (j + 1) * SUBLANES, :]
        h, p = [reg(v, 0)], [reg(a, 0)]
        for j in range(1, SUBLANES):
            h.append(reg(a, j) * h[-1] + reg(v, j))
            p.append(reg(a, j) * p[-1])
        hc, pc = h[-1], p[-1]
        k = 1
        while k < SUBLANES:
            m = q >= k
            hc = jnp.where(m, pc * pltpu.roll(hc, k, 0) + hc, hc)
            pc = jnp.where(m, pc * pltpu.roll(pc, k, 0), pc)
            k *= 2
        end = hc + pc * cin
        carry = jnp.where(q == 0, cin, pltpu.roll(end, 1, 0))
        cin = end[SUBLANES - 1:SUBLANES, :]
        for j in range(SUBLANES):
            full = h[j] + p[j] * carry
            for s in range(slabs):
                hs[s, pl.ds(u * UNIT + j, SUBLANES, stride=SUBLANES), :] = full[:, s * LANES:(s + 1) * LANES]
    hcar[...] = jnp.broadcast_to(cin, hcar.shape)

    for s in range(slabs):
        ls = slice(s * LANES, (s + 1) * LANES)
        o_ref[:, ls] = (hs[s] * _gelu_tanh(y_ref[:, ls].astype(F32))).astype(o_ref.dtype)


def _lru_branch(proj, cols, bsz, seq, conv_w, conv_b, wa_grp, ba, wi_grp, bi, lam, tile):
    c = conv_w.shape[1]
    assert c % LANES == 0 and tile % UNIT == 0
    nt = seq // tile
    xcol, ycol = cols["lru_x"] // c, cols["lru_y"] // c
    full = lambda b, t: (0, 0)
    wspec = pl.BlockSpec(wa_grp.shape, lambda b, t: (0, 0, 0))
    return pl.pallas_call(
        functools.partial(_lru_kernel, tile=tile),
        grid=(bsz, nt),
        in_specs=[pl.BlockSpec((tile, c), lambda b, t: (b * nt + t, xcol)),
                  pl.BlockSpec((tile, c), lambda b, t: (b * nt + t, ycol)),
                  pl.BlockSpec((CONV_WIDTH, c), full),
                  pl.BlockSpec((1, c), full),
                  wspec,
                  pl.BlockSpec((1, c), full),
                  wspec,
                  pl.BlockSpec((1, c), full),
                  pl.BlockSpec((1, c), full)],
        out_specs=pl.BlockSpec((tile, c), lambda b, t: (b * nt + t, 0)),
        out_shape=jax.ShapeDtypeStruct((bsz * seq, c), BF16),
        scratch_shapes=[pltpu.VMEM((c // LANES, tile + SUBLANES, LANES), F32),
                        pltpu.VMEM((c // LANES, tile, LANES), F32),
                        pltpu.VMEM((SUBLANES, c), F32)],
        compiler_params=_params("parallel", "arbitrary"),
        name="rg_lru",
    )(proj, proj, conv_w, conv_b, wa_grp, ba, wi_grp, bi, lam)


def _gla_kernel(q_ref, k_ref, v_ref, r_ref, sm_ref, wa2_ref, ba2_ref, ng_ref, o_ref, st_ref,
                *, tile, heads, dk, dv):
    t = pl.program_id(1)

    @pl.when(t == 0)
    def _():
        st_ref[...] = jnp.zeros(st_ref.shape, F32)

    z = _dot(sm_ref[...].astype(BF16), wa2_ref[...]) + ba2_ref[...]
    gk = (jnp.minimum(z, 0.0) - jnp.log1p(jnp.exp(-jnp.abs(z)))) / GLA_TAU
    b = _group_cumsum(gk, CHUNK)
    row = lax.broadcasted_iota(jnp.int32, (CHUNK, CHUNK), 0)
    col = lax.broadcasted_iota(jnp.int32, (CHUNK, CHUNK), 1)
    causal = row >= col
    scale = dk ** -0.5
    ng = ng_ref[...]
    chunks = tile // CHUNK
    heads_r = range(heads)
    ksl = [slice(h * dk, (h + 1) * dk) for h in heads_r]
    vsl = [slice(h * dv, (h + 1) * dv) for h in heads_r]
    qes, decs, intra, upd = [], [], [], []
    for c in range(chunks):
        rows = slice(c * CHUNK, (c + 1) * CHUNK)
        bc = b[rows]
        bl = bc[CHUNK - 1:CHUNK]
        qf = q_ref[rows, :].astype(F32)
        kf = k_ref[rows, :].astype(F32)
        qe = ((qf * scale) * jnp.exp(bc)).astype(BF16)
        ke = (kf * jnp.exp(-bc)).astype(BF16)
        kd = (kf * jnp.exp(bl - bc)).astype(BF16)
        qes.append(qe)
        decs.append(jnp.exp(bl))
        att = [jnp.where(causal, lax.dot_general(qe[:, ksl[h]], ke[:, ksl[h]], _NT, preferred_element_type=F32),
                         0.0).astype(BF16) for h in heads_r]
        intra.append([_dot(att[h], v_ref[rows, vsl[h]]) for h in heads_r])
        upd.append([lax.dot_general(v_ref[rows, vsl[h]], kd[:, ksl[h]], _TN, preferred_element_type=F32)
                    for h in heads_r])
    states = [st_ref[h] for h in heads_r]
    for c in range(chunks):
        rows = slice(c * CHUNK, (c + 1) * CHUNK)
        for h in heads_r:
            o = intra[c][h] + lax.dot_general(qes[c][:, ksl[h]], states[h].astype(BF16), _NT,
                                              preferred_element_type=F32)
            states[h] = states[h] * decs[c][:, ksl[h]] + upd[c][h]
            o = o * lax.rsqrt(jnp.mean(o * o, axis=-1, keepdims=True) + NORM_EPS) * ng
            o_ref[rows, vsl[h]] = (o * _silu(r_ref[rows, vsl[h]].astype(F32))).astype(o_ref.dtype)
    for h in heads_r:
        st_ref[h] = states[h]


def _gla_branch(proj, small, cols, bsz, seq, wa2_pad, ba2, norm_g, tile):
    hdk = wa2_pad.shape[1]
    dv = norm_g.shape[1]
    heads = 4
    dk = hdk // heads
    hdv = heads * dv
    nt = seq // tile
    full = lambda b, t: (0, 0)
    qcol, kcol = cols["gla_q"] // hdk, cols["gla_k"] // hdk
    vcol, rcol = cols["gla_v"] // hdv, cols["gla_r"] // hdv
    return pl.pallas_call(
        functools.partial(_gla_kernel, tile=tile, heads=heads, dk=dk, dv=dv),
        grid=(bsz, nt),
        in_specs=[pl.BlockSpec((tile, hdk), lambda b, t: (b * nt + t, qcol)),
                  pl.BlockSpec((tile, hdk), lambda b, t: (b * nt + t, kcol)),
                  pl.BlockSpec((tile, hdv), lambda b, t: (b * nt + t, vcol)),
                  pl.BlockSpec((tile, hdv), lambda b, t: (b * nt + t, rcol)),
                  pl.BlockSpec((tile, LANES), lambda b, t: (b * nt + t, 0)),
                  pl.BlockSpec((LANES, hdk), full),
                  pl.BlockSpec((1, hdk), full),
                  pl.BlockSpec((1, dv), full)],
        out_specs=pl.BlockSpec((tile, hdv), lambda b, t: (b * nt + t, 0)),
        out_shape=jax.ShapeDtypeStruct((bsz * seq, hdv), BF16),
        scratch_shapes=[pltpu.VMEM((heads, dv, dk), F32)],
        compiler_params=_params("parallel", "arbitrary"),
        name="gla",
    )(proj, proj, proj, proj, small, wa2_pad, ba2, norm_g)


def _pair_blockdiag(y, lo):
    return jnp.concatenate([jnp.where(lo, y, 0.0), jnp.where(lo, 0.0, y)], axis=0).astype(BF16)


def _unit_lower_inverse_minus_eye(mats, row, col, lo):
    def mul(xs, ys):
        return [jnp.dot(x.astype(BF16), _pair_blockdiag(y, lo), preferred_element_type=F32)
                for x, y in zip(xs, ys)]

    eye = (row == col).astype(F32)
    blk16 = (row // 16) == (col // 16)
    blk32 = (row // 32) == (col // 32)
    off16 = jnp.logical_and(blk32, jnp.logical_not(blk16))
    d = [jnp.where(blk16, a, 0.0) for a in mats]
    d2 = mul(d, d)
    d4 = mul(d2, d2)
    d8 = mul(d4, d4)
    t = mul([eye - x for x in d], [eye + x for x in d2])
    t = mul(t, [eye + x for x in d4])
    t = mul(t, [eye + x for x in d8])
    c1 = [jnp.where(off16, a, 0.0) for a in mats]
    u = mul(mul(t, c1), t)
    t = [x - y for x, y in zip(t, u)]
    c2 = [jnp.where(blk32, 0.0, a) for a in mats]
    u = mul(mul(t, c2), t)
    return [x - y - eye for x, y in zip(t, u)]


def _dn_kernel(qkv_ref, z_ref, sm_ref, cw_ref, alog_ref, dtb_ref, ng_ref, o_ref,
               xbuf, act, gsc, bsc, val_s, kc_s, qg_s, kg_s, aq_s, st_ref,
               *, tile, heads, dk, dv, b_lane, a_lane, group):
    t = pl.program_id(1)
    _load_conv_tile(qkv_ref, xbuf, t, tile)

    @pl.when(t == 0)
    def _():
        st_ref[...] = jnp.zeros(st_ref.shape, F32)

    hk = heads * dk
    for h in range(heads):
        for part, scale in ((0, dk ** -0.5), (1, 1.0)):
            cs = slice(part * hk + h * dk, part * hk + (h + 1) * dk)
            x = _silu(_conv_cols(xbuf, cw_ref, cs, tile))
            act[:, cs] = x * (lax.rsqrt(jnp.sum(x * x, axis=-1, keepdims=True) + NORM_EPS) * scale)
    vs_all = slice(2 * hk, 2 * hk + heads * dv)
    act[:, vs_all] = _silu(_conv_cols(xbuf, cw_ref, vs_all, tile))

    sm = sm_ref[...]
    bsc[...] = _sigmoid(sm)
    gsc[...] = _group_cumsum(-jnp.exp(alog_ref[...]) * _softplus(sm + dtb_ref[...]), CHUNK)

    row = lax.broadcasted_iota(jnp.int32, (CHUNK, LANES), 0)
    lane = lax.broadcasted_iota(jnp.int32, (CHUNK, LANES), 1)
    col = lane & (CHUNK - 1)
    lo = lane < CHUNK
    lo_row = lo[0:1]
    incl = row >= col
    strict = row > col
    first = lax.broadcasted_iota(jnp.int32, (CHUNK, 2 * dk), 1) < dk
    ng = ng_ref[...]
    pairs = heads // 2

    def pick(arr, l0, wide):
        return jnp.where(first if wide else lo, arr[:, l0:l0 + 1], arr[:, l0 + 1:l0 + 2])

    def prepare(gidx, carry):
        low, rhs2 = [], []
        for cc in range(group):
            r0 = (gidx * group + cc) * CHUNK
            rows = pl.ds(r0, CHUNK)
            gc = gsc[rows, :]
            be = bsc[rows, :]
            eg = jnp.exp(gc)
            egl = jnp.exp(gc[CHUNK - 1:CHUNK, :] - gc)
            gct = jnp.concatenate([gc, gc], axis=0).T
            for p in range(pairs):
                h0 = 2 * p
                q2 = act[rows, h0 * dk:(h0 + 2) * dk]
                k2 = act[rows, hk + h0 * dk:hk + (h0 + 2) * dk]
                v2 = act[rows, 2 * hk + h0 * dv:2 * hk + (h0 + 2) * dv]
                bcol = pick(be, b_lane + h0, True)
                egc = pick(eg, a_lane + h0, True)
                kb2 = k2 * bcol
                lhs = jnp.concatenate([kb2, q2], axis=0).astype(BF16)
                kbd = jnp.concatenate([jnp.where(first, k2, 0.0), jnp.where(first, 0.0, k2)],
                                      axis=0).astype(BF16)
                both = lax.dot_general(lhs, kbd, _NT, preferred_element_type=F32)
                la = a_lane + h0
                diff = pick(gc, la, False) - jnp.where(lo_row, gct[la:la + 1, :], gct[la + 1:la + 2, :])
                decay = jnp.where(incl, jnp.exp(jnp.where(incl, diff, 0.0)), 0.0)
                low.append(jnp.where(strict, both[:CHUNK] * decay, 0.0))
                aqk = both[CHUNK:] * decay
                aq_s[rows, p * 2 * LANES:(p * 2 + 1) * LANES] = jnp.where(lo, aqk, 0.0).astype(BF16)
                aq_s[rows, (p * 2 + 1) * LANES:(p * 2 + 2) * LANES] = jnp.where(lo, 0.0, aqk).astype(BF16)
                vb, kbe = v2 * bcol, kb2 * egc
                rhs2.append((rows, h0, jnp.concatenate(
                    [jnp.concatenate([vb[:, :dv], kbe[:, :dk]], axis=1),
                     jnp.concatenate([vb[:, dv:], kbe[:, dk:]], axis=1)], axis=0)))
                qg_s[rows, h0 * dk:(h0 + 2) * dk] = (q2 * egc).astype(BF16)
                kg_s[rows, h0 * dk:(h0 + 2) * dk] = (k2 * pick(egl, a_lane + h0, True)).astype(BF16)
        nmat = _unit_lower_inverse_minus_eye(low, row, col, lo)
        for n, (rows, h0, rhs) in zip(nmat, rhs2):
            sol = rhs + _dot(_pair_blockdiag(n, lo), rhs.astype(BF16))
            for j in range(2):
                h = h0 + j
                val_s[rows, h * dv:(h + 1) * dv] = sol[j * CHUNK:(j + 1) * CHUNK, :dv]
                kc_s[rows, h * dk:(h + 1) * dk] = sol[j * CHUNK:(j + 1) * CHUNK, dv:].astype(BF16)
        return carry

    for gidx in range(tile // (CHUNK * group)):
        prepare(gidx, 0)

    def recur(c, carry):
        r0 = c * CHUNK
        rows = pl.ds(r0, CHUNK)
        edl = jnp.exp(gsc[rows, :][CHUNK - 1:CHUNK, :])
        states = [st_ref[h] for h in range(heads)]
        both = [_dot(jnp.concatenate([kc_s[rows, h * dk:(h + 1) * dk], qg_s[rows, h * dk:(h + 1) * dk]], axis=0),
                     states[h].astype(BF16)) for h in range(heads)]
        vnb = [(val_s[rows, h * dv:(h + 1) * dv] - both[h][:CHUNK]).astype(BF16) for h in range(heads)]
        intra = [_dot(jnp.concatenate([aq_s[rows, p * 2 * LANES:(p * 2 + 1) * LANES],
                                       aq_s[rows, (p * 2 + 1) * LANES:(p * 2 + 2) * LANES]], axis=0),
                      jnp.concatenate([vnb[2 * p], vnb[2 * p + 1]], axis=0)) for p in range(pairs)]
        for h in range(heads):
            la = a_lane + h
            st_ref[h] = states[h] * edl[:, la:la + 1] + lax.dot_general(
                kg_s[rows, h * dk:(h + 1) * dk], vnb[h], _TN, preferred_element_type=F32)
        for h in range(heads):
            j = h % 2
            o = both[h][CHUNK:] + intra[h // 2][j * CHUNK:(j + 1) * CHUNK]
            o = o * lax.rsqrt(jnp.mean(o * o, axis=-1, keepdims=True) + NORM_EPS) * ng
            zc = z_ref[rows, h * dv:(h + 1) * dv].astype(F32)
            o_ref[rows, h * dv:(h + 1) * dv] = (o * _silu(zc)).astype(o_ref.dtype)
        return carry

    for c in range(tile // CHUNK):
        recur(c, 0)


def _conv_cols(xbuf, cw_ref, cs, rows):
    acc = None
    for k in range(CONV_WIDTH):
        off = SUBLANES - (CONV_WIDTH - 1) + k
        term = cw_ref[k:k + 1, cs] * xbuf[off:off + rows, cs]
        acc = term if acc is None else acc + term
    return acc


def _dn_branch(proj, small, cols, bsz, seq, conv_w, alog_pad, dtb_pad, norm_g, heads, b_lane, a_lane, tile):
    cq = conv_w.shape[1]
    dv = norm_g.shape[1]
    dk = (cq // heads - dv) // 2
    assert dk == dv == LANES and 2 * CHUNK == LANES and heads % 2 == 0
    hdv = heads * dv
    nt = seq // tile
    full = lambda b, t: (0, 0)
    qcol, zcol = cols["dn_qkv"] // cq, cols["dn_z"] // hdv
    group = tile // CHUNK
    return pl.pallas_call(
        functools.partial(_dn_kernel, tile=tile, heads=heads, dk=dk, dv=dv, b_lane=b_lane, a_lane=a_lane,
                          group=group),
        grid=(bsz, nt),
        in_specs=[pl.BlockSpec((tile, cq), lambda b, t: (b * nt + t, qcol)),
                  pl.BlockSpec((tile, hdv), lambda b, t: (b * nt + t, zcol)),
                  pl.BlockSpec((tile, LANES), lambda b, t: (b * nt + t, 0)),
                  pl.BlockSpec((CONV_WIDTH, cq), full),
                  pl.BlockSpec((1, LANES), full),
                  pl.BlockSpec((1, LANES), full),
                  pl.BlockSpec((1, dv), full)],
        out_specs=pl.BlockSpec((tile, hdv), lambda b, t: (b * nt + t, 0)),
        out_shape=jax.ShapeDtypeStruct((bsz * seq, hdv), BF16),
        scratch_shapes=[pltpu.VMEM((tile + SUBLANES, cq), F32),
                        pltpu.VMEM((tile, cq), F32),
                        pltpu.VMEM((tile, LANES), F32),
                        pltpu.VMEM((tile, LANES), F32),
                        pltpu.VMEM((tile, hdv), F32),
                        pltpu.VMEM((tile, heads * dk), BF16),
                        pltpu.VMEM((tile, heads * dk), BF16),
                        pltpu.VMEM((tile, heads * dk), BF16),
                        pltpu.VMEM((tile, heads * LANES), BF16),
                        pltpu.VMEM((heads, dk, dv), F32)],
        compiler_params=_params("parallel", "arbitrary"),
        name="gated_deltanet",
    )(proj, proj, small, conv_w, alog_pad, dtb_pad, norm_g)


def _merge_kernel(y1_ref, y2_ref, y3_ref, gl_ref, bg_ref, wb_ref, o_ref):
    d = o_ref.shape[1]
    rb = o_ref.shape[0] // ROW_BLOCKS
    for h in range(ROW_BLOCKS):
        rows = slice(h * rb, (h + 1) * rb)
        acc = None
        for i, y_ref in enumerate((y1_ref, y2_ref, y3_ref)):
            gate = _sigmoid(gl_ref[rows, i * d:(i + 1) * d].astype(F32) + bg_ref[:, i * d:(i + 1) * d])
            term = gate * _dot(y_ref[rows, :], wb_ref[i])
            acc = term if acc is None else acc + term
        o_ref[rows, :] = acc.astype(o_ref.dtype)


def _merge(ys, proj, cols, b_gate, w_branch, tm):
    m, c = ys[0].shape
    nb, _, d = w_branch.shape
    gcol = cols["gate"] // (nb * d)
    yspec = pl.BlockSpec((tm, c), lambda i: (i, 0))
    return pl.pallas_call(
        _merge_kernel,
        grid=(m // tm,),
        in_specs=[yspec, yspec, yspec,
                  pl.BlockSpec((tm, nb * d), lambda i: (i, gcol)),
                  pl.BlockSpec((1, nb * d), lambda i: (0, 0)),
                  pl.BlockSpec((nb, c, d), lambda i: (0, 0, 0), pipeline_mode=pl.Buffered(1))],
        out_specs=pl.BlockSpec((tm, d), lambda i: (i, 0)),
        out_shape=jax.ShapeDtypeStruct((m, d), BF16),
        compiler_params=_params("parallel"),
        name="branch_merge",
    )(*ys, proj, b_gate, w_branch)


def _layernorm_store(y, g_ref, b_ref, of_ref, ob_ref, rows=slice(None)):
    mu = jnp.mean(y, axis=-1, keepdims=True)
    yc = y - mu
    var = jnp.mean(yc * yc, axis=-1, keepdims=True)
    out = yc * lax.rsqrt(var + LN_EPS) * g_ref[...] + b_ref[...]
    of_ref[rows, :] = out
    ob_ref[rows, :] = out.astype(ob_ref.dtype)


def _outln_kernel(m_ref, w_ref, x_ref, g_ref, b_ref, of_ref, ob_ref, *, alpha):
    rb = x_ref.shape[0] // ROW_BLOCKS
    for h in range(ROW_BLOCKS):
        rows = slice(h * rb, (h + 1) * rb)
        y = alpha * x_ref[rows, :] + _dot(m_ref[rows, :], w_ref[...])
        _layernorm_store(y, g_ref, b_ref, of_ref, ob_ref, rows)


def _out_ln(merged, w_out, x, g, b, alpha, tm):
    m, d = x.shape
    row = pl.BlockSpec((tm, d), lambda i: (i, 0))
    vec = pl.BlockSpec((1, d), lambda i: (0, 0))
    return pl.pallas_call(
        functools.partial(_outln_kernel, alpha=alpha),
        grid=(m // tm,),
        in_specs=[row, pl.BlockSpec((d, d), lambda i: (0, 0), pipeline_mode=pl.Buffered(1)), row, vec, vec],
        out_specs=[row, row],
        out_shape=[jax.ShapeDtypeStruct((m, d), F32), jax.ShapeDtypeStruct((m, d), BF16)],
        compiler_params=_params("parallel"),
        name="out_proj_ln",
    )(merged, w_out, x, g, b)


def _mlp_kernel(xb_ref, xf_ref, w1_ref, b1_ref, w2_ref, b2_ref, g_ref, b_ref, of_ref, ob_ref, acc_ref,
                *, alpha):
    f = pl.program_id(1)

    @pl.when(f == 0)
    def _():
        acc_ref[...] = jnp.zeros(acc_ref.shape, F32)

    rb = xb_ref.shape[0] // ROW_BLOCKS
    hidden = []
    for h in range(ROW_BLOCKS):
        a = jnp.maximum(_dot(xb_ref[h * rb:(h + 1) * rb, :], w1_ref[...]) + b1_ref[...], 0.0)
        hidden.append((a * a).astype(BF16))
    for h in range(ROW_BLOCKS):
        acc_ref[h * rb:(h + 1) * rb, :] += _dot(hidden[h], w2_ref[...])

    @pl.when(f == pl.num_programs(1) - 1)
    def _():
        y = alpha * xf_ref[...] + (acc_ref[...] + b2_ref[...])
        _layernorm_store(y, g_ref, b_ref, of_ref, ob_ref)


def _mlp_ln(xb, xf, w1, b1, w2, b2, g, b, alpha, tm, tf):
    m, d = xf.shape
    ff = w1.shape[1]
    row = pl.BlockSpec((tm, d), lambda i, f: (i, 0))
    vec = pl.BlockSpec((1, d), lambda i, f: (0, 0))
    return pl.pallas_call(
        functools.partial(_mlp_kernel, alpha=alpha),
        grid=(m // tm, ff // tf),
        in_specs=[row, row,
                  pl.BlockSpec((d, tf), lambda i, f: (0, f)),
                  pl.BlockSpec((1, tf), lambda i, f: (0, f)),
                  pl.BlockSpec((tf, d), lambda i, f: (f, 0)),
                  vec, vec, vec],
        out_specs=[row, row],
        out_shape=[jax.ShapeDtypeStruct((m, d), F32), jax.ShapeDtypeStruct((m, d), BF16)],
        scratch_shapes=[pltpu.VMEM((tm, d), F32)],
        compiler_params=_params("parallel", "arbitrary"),
        name="mlp_ln",
    )(xb, xf, w1, b1, w2, b2, g, b)


def _column_plan(sizes):
    order = ("dn_qkv", "lru_x", "lru_y", "gla_v", "gate", "gla_r", "dn_z", "gla_q", "gla_k")
    cols, off = {}, 0
    for name in order:
        assert off % sizes[name] == 0, (name, off, sizes[name])
        cols[name] = off
        off += sizes[name]
    return order, cols, off


def _group_block_diag(w, width=2 * LANES):
    l, g, n, _ = w.shape
    per = width // n
    eye = jnp.eye(per, dtype=w.dtype)
    wg = w.reshape(l, g // per, per, n, n)
    return (eye[None, None, :, None, :, None] * wg[:, :, :, :, None, :]).reshape(l, g // per, width, width)


def kernel(x, w_in, lru_conv_w, lru_conv_b, lru_wa, lru_ba, lru_wi, lru_bi, lru_lambda, gla_wa2, gla_ba2, gla_norm_g, dn_conv_w, dn_a_log, dn_dt_bias, dn_norm_g, w_branch, b_gate, w_out, ln1_g, ln1_b, mlp_w1, mlp_b1, mlp_w2, mlp_b2, ln2_g, ln2_b):
    bsz, seq, d = x.shape
    depth = w_in.shape[0]
    m = bsz * seq
    alpha = (2.0 * depth) ** 0.25

    lru_w = lru_conv_w.shape[-1]
    gla_heads = 4
    gla_rank, gla_hdk = gla_wa2.shape[1], gla_wa2.shape[2]
    gla_hdv = gla_heads * gla_norm_g.shape[-1]
    dn_heads = dn_a_log.shape[-1]
    dn_cq = dn_conv_w.shape[-1]
    dn_hdv = dn_heads * dn_norm_g.shape[-1]
    nb = w_branch.shape[1]
    sizes = {"lru_x": lru_w, "lru_y": lru_w, "gla_q": gla_hdk, "gla_k": gla_hdk, "gla_v": gla_hdv,
             "gla_alr": gla_rank, "gla_r": gla_hdv, "dn_qkv": dn_cq, "dn_b": dn_heads, "dn_a": dn_heads,
             "dn_z": dn_hdv, "gate": nb * d}
    ref_order = ("lru_x", "lru_y", "gla_q", "gla_k", "gla_v", "gla_alr", "gla_r",
                 "dn_qkv", "dn_b", "dn_a", "dn_z", "gate")
    src, off = {}, 0
    for name in ref_order:
        src[name] = off
        off += sizes[name]
    assert off == w_in.shape[-1]
    order, cols, n_big = _column_plan(sizes)

    take = lambda name: w_in[:, :, src[name]:src[name] + sizes[name]]
    w_big = jnp.concatenate([take(n) for n in order], axis=-1).astype(BF16)
    b_lane, a_lane = gla_rank, gla_rank + dn_heads
    n_small = gla_rank + 2 * dn_heads
    w_small = jnp.concatenate([take("gla_alr"), take("dn_b"), take("dn_a"),
                               jnp.zeros((depth, d, LANES - n_small), w_in.dtype)], axis=-1).astype(BF16)
    wa2_pad = jnp.concatenate([gla_wa2, jnp.zeros((depth, LANES - gla_rank, gla_hdk), gla_wa2.dtype)],
                              axis=1).astype(BF16)

    def lane_pad(p, lane):
        return jnp.pad(p, ((0, 0), (lane, LANES - lane - p.shape[1])))[:, None, :]

    alog_pad = lane_pad(dn_a_log, a_lane)
    dtb_pad = lane_pad(dn_dt_bias, a_lane)
    wa_grp = _group_block_diag(lru_wa).astype(BF16)
    wi_grp = _group_block_diag(lru_wi).astype(BF16)
    wb16, wo16 = w_branch.astype(BF16), w_out.astype(BF16)
    w1_16, w2_16 = mlp_w1.astype(BF16), mlp_w2.astype(BF16)
    vec = lambda p, l: p[l][None, :]

    tile = min(256, seq)
    xf = x.reshape(m, d)
    xb = xf.astype(BF16)
    for l in range(depth):
        proj = _matmul(xb, w_big[l], BF16, min(1024, m), 1536 if n_big % 1536 == 0 else n_big, "in_proj")
        small = _matmul(xb, w_small[l], F32, min(1024, m), LANES, "in_proj_small")
        y_lru = _lru_branch(proj, cols, bsz, seq, lru_conv_w[l], vec(lru_conv_b, l), wa_grp[l],
                            vec(lru_ba, l), wi_grp[l], vec(lru_bi, l), vec(lru_lambda, l), tile)
        y_gla = _gla_branch(proj, small, cols, bsz, seq, wa2_pad[l], vec(gla_ba2, l), vec(gla_norm_g, l), tile)
        y_dn = _dn_branch(proj, small, cols, bsz, seq, dn_conv_w[l], alog_pad[l], dtb_pad[l],
                          vec(dn_norm_g, l), dn_heads, b_lane, a_lane, tile)
        merged = _merge((y_lru, y_gla, y_dn), proj, cols, b_gate[l].reshape(1, nb * d), wb16[l], min(512, m))
        xf, xb = _out_ln(merged, wo16[l], xf, vec(ln1_g, l), vec(ln1_b, l), alpha, min(512, m))
        xf, xb = _mlp_ln(xb, xf, w1_16[l], vec(mlp_b1, l), w2_16[l], vec(mlp_b2, l),
                         vec(ln2_g, l), vec(ln2_b, l), alpha, min(512, m), 1024)
    return xf.reshape(bsz, seq, d)
```

```python
import functools

import jax
import jax.numpy as jnp
from jax import lax
from jax.experimental import pallas as pl
from jax.experimental.pallas import tpu as pltpu

F32 = jnp.float32
BF16 = jnp.bfloat16

CONV_WIDTH = 4
CHUNK = 64
LRU_C = 8.0
GLA_TAU = 16.0
LN_EPS = 1e-5
NORM_EPS = 1e-6

SUBLANES = 8
LANES = 128
VMEM_LIMIT_BYTES = 56 * 1024 * 1024
ROW_BLOCKS = 2

_NT = (((1,), (1,)), ((), ()))
_TN = (((0,), (0,)), ((), ()))


def _params(*sem):
    return pltpu.CompilerParams(dimension_semantics=sem, vmem_limit_bytes=VMEM_LIMIT_BYTES)


def _sigmoid(x):
    return 0.5 * (1.0 + jnp.tanh(0.5 * x))


def _softplus(x):
    return jnp.maximum(x, 0.0) + jnp.log1p(jnp.exp(-jnp.abs(x)))


def _silu(x):
    h = 0.5 * x
    return h + h * jnp.tanh(h)


def _gelu_tanh(x):
    return x * (0.5 * (1.0 + jnp.tanh(0.7978845608028654 * (x + 0.044715 * (x * x * x)))))


def _dot(a, b):
    return jnp.dot(a, b, preferred_element_type=F32)


def _group_cumsum(x, period):
    pos = lax.broadcasted_iota(jnp.int32, x.shape, 0) & (period - 1)
    k = 1
    while k < period:
        x = x + jnp.where(pos >= k, pltpu.roll(x, k, 0), 0.0)
        k *= 2
    return x


def _mm_kernel(x_ref, w_ref, o_ref):
    o_ref[...] = _dot(x_ref[...], w_ref[...]).astype(o_ref.dtype)


def _matmul(x, w, layer, out_dtype, tm, tn, name):
    m, k = x.shape
    n = w.shape[2]
    assert m % tm == 0 and n % tn == 0
    return pl.pallas_call(
        _mm_kernel,
        grid=(n // tn, m // tm),
        in_specs=[pl.BlockSpec((tm, k), lambda j, i: (i, 0)),
                  pl.BlockSpec((None, k, tn), lambda j, i: (layer, 0, j))],
        out_specs=pl.BlockSpec((tm, tn), lambda j, i: (i, j)),
        out_shape=jax.ShapeDtypeStruct((m, n), out_dtype),
        compiler_params=_params("parallel", "parallel"),
        name=name,
    )(x, w)


UNIT = SUBLANES * SUBLANES


def _lru_kernel(x_ref, y_ref, cw_ref, cb_ref, wa_ref, ba_ref, wi_ref, bi_ref, lam_ref,
                o_ref, xs, hs, hcar, *, tile):
    t = pl.program_id(1)
    slabs = xs.shape[0]
    c = slabs * LANES
    groups = wa_ref.shape[0]
    gw = c // groups

    @pl.when(t == 0)
    def _():
        hcar[...] = jnp.zeros(hcar.shape, F32)
        for s in range(slabs):
            xs[s, 0:SUBLANES, :] = jnp.zeros((SUBLANES, LANES), F32)

    @pl.when(t > 0)
    def _():
        for s in range(slabs):
            xs[s, 0:SUBLANES, :] = xs[s, tile:tile + SUBLANES, :]

    for s in range(slabs):
        xs[s, SUBLANES:SUBLANES + tile, :] = x_ref[:, s * LANES:(s + 1) * LANES].astype(F32)

    units = tile // UNIT
    taps = CONV_WIDTH - 1
    xc_units = []
    for u in range(units):
        per_slab = []
        for s in range(slabs):
            ls = slice(s * LANES, (s + 1) * LANES)
            x = {j: xs[s, pl.ds(SUBLANES + u * UNIT + j, SUBLANES, stride=SUBLANES), :]
                 for j in range(-taps, SUBLANES)}
            regs = []
            for j in range(SUBLANES):
                acc = cb_ref[:, ls] + cw_ref[taps:taps + 1, ls] * x[j]
                for k in range(taps):
                    acc = acc + cw_ref[k:k + 1, ls] * x[j - taps + k]
                regs.append(acc)
            per_slab.append(jnp.concatenate(regs, axis=0))
        xc_units.append(jnp.concatenate(per_slab, axis=1))
    xc = jnp.concatenate(xc_units, axis=0)
    xcb = xc.astype(BF16)

    def gate(w_ref, b_ref):
        pre = jnp.concatenate([_dot(xcb[:, g * gw:(g + 1) * gw], w_ref[g]) for g in range(groups)], axis=1)
        return _sigmoid(pre + b_ref[...])

    r = gate(wa_ref, ba_ref)
    i = gate(wi_ref, bi_ref)
    log_a = (-LRU_C * _softplus(-lam_ref[...])) * r
    a = jnp.exp(log_a)
    v = jnp.sqrt(-jnp.tanh(log_a) * (a * a + 1.0)) * (i * xc)

    q = lax.broadcasted_iota(jnp.int32, (SUBLANES, c), 0)
    cin = hcar[0:1, :]
    for u in range(units):
        reg = lambda arr, j: arr[u * UNIT + j * SUBLANES:u * UNIT + (j + 1) * SUBLANES, :]
        h, p = [reg(v, 0)], [reg(a, 0)]
        for j in range(1, SUBLANES):
            h.append(reg(a, j) * h[-1] + reg(v, j))
            p.append(reg(a, j) * p[-1])
        hc, pc = h[-1], p[-1]
        k = 1
        while k < SUBLANES:
            m = q >= k
            hc = jnp.where(m, pc * pltpu.roll(hc, k, 0) + hc, hc)
            pc = jnp.where(m, pc * pltpu.roll(pc, k, 0), pc)
            k *= 2
        end = hc + pc * cin
        carry = jnp.where(q == 0, cin, pltpu.roll(end, 1, 0))
        cin = end[SUBLANES - 1:SUBLANES, :]
        for j in range(SUBLANES):
            full = h[j] + p[j] * carry
            for s in range(slabs):
                hs[s, pl.ds(u * UNIT + j, SUBLANES, stride=SUBLANES), :] = full[:, s * LANES:(s + 1) * LANES]
    hcar[...] = jnp.broadcast_to(cin, hcar.shape)

    for s in range(slabs):
        ls = slice(s * LANES, (s + 1) * LANES)
        o_ref[:, ls] = (hs[s] * _gelu_tanh(y_ref[:, ls].astype(F32))).astype(o_ref.dtype)


def _lru_branch(proj, cols, bsz, seq, conv_w, conv_b, wa_grp, ba, wi_grp, bi, lam, tile):
    c = conv_w.shape[1]
    assert c % LANES == 0 and tile % UNIT == 0
    nt = seq // tile
    xcol, ycol = cols["lru_x"] // c, cols["lru_y"] // c
    full = lambda b, t: (0, 0)
    wspec = pl.BlockSpec(wa_grp.shape, lambda b, t: (0, 0, 0))
    return pl.pallas_call(
        functools.partial(_lru_kernel, tile=tile),
        grid=(bsz, nt),
        in_specs=[pl.BlockSpec((tile, c), lambda b, t: (b * nt + t, xcol)),
                  pl.BlockSpec((tile, c), lambda b, t: (b * nt + t, ycol)),
                  pl.BlockSpec((CONV_WIDTH, c), full),
                  pl.BlockSpec((1, c), full),
                  wspec,
                  pl.BlockSpec((1, c), full),
                  wspec,
                  pl.BlockSpec((1, c), full),
                  pl.BlockSpec((1, c), full)],
        out_specs=pl.BlockSpec((tile, c), lambda b, t: (b * nt + t, 0)),
        out_shape=jax.ShapeDtypeStruct((bsz * seq, c), BF16),
        scratch_shapes=[pltpu.VMEM((c // LANES, tile + SUBLANES, LANES), F32),
                        pltpu.VMEM((c // LANES, tile, LANES), F32),
                        pltpu.VMEM((SUBLANES, c), F32)],
        compiler_params=_params("parallel", "arbitrary"),
        name="rg_lru",
    )(proj["lru_x"], proj["lru_y"], conv_w, conv_b, wa_grp, ba, wi_grp, bi, lam)


def _gla_kernel(q_ref, k_ref, v_ref, r_ref, sm_ref, wa2_ref, ba2_ref, ng_ref, o_ref, st_ref,
                *, tile, heads, dk, dv):
    t = pl.program_id(1)

    @pl.when(t == 0)
    def _():
        st_ref[...] = jnp.zeros(st_ref.shape, F32)

    z = _dot(sm_ref[...].astype(BF16), wa2_ref[...]) + ba2_ref[...]
    gk = (jnp.minimum(z, 0.0) - jnp.log1p(jnp.exp(-jnp.abs(z)))) / GLA_TAU
    b = _group_cumsum(gk, CHUNK)
    row = lax.broadcasted_iota(jnp.int32, (CHUNK, CHUNK), 0)
    col = lax.broadcasted_iota(jnp.int32, (CHUNK, CHUNK), 1)
    causal = row >= col
    scale = dk ** -0.5
    ng = ng_ref[...]
    chunks = tile // CHUNK
    heads_r = range(heads)
    ksl = [slice(h * dk, (h + 1) * dk) for h in heads_r]
    vsl = [slice(h * dv, (h + 1) * dv) for h in heads_r]
    qes, decs, intra, upd = [], [], [], []
    for c in range(chunks):
        rows = slice(c * CHUNK, (c + 1) * CHUNK)
        bc = b[rows]
        bl = bc[CHUNK - 1:CHUNK]
        qf = q_ref[rows, :].astype(F32)
        kf = k_ref[rows, :].astype(F32)
        qe = ((qf * scale) * jnp.exp(bc)).astype(BF16)
        ke = (kf * jnp.exp(-bc)).astype(BF16)
        kd = (kf * jnp.exp(bl - bc)).astype(BF16)
        qes.append(qe)
        decs.append(jnp.exp(bl))
        att = [jnp.where(causal, lax.dot_general(qe[:, ksl[h]], ke[:, ksl[h]], _NT, preferred_element_type=F32),
                         0.0).astype(BF16) for h in heads_r]
        intra.append([_dot(att[h], v_ref[rows, vsl[h]]) for h in heads_r])
        upd.append([lax.dot_general(v_ref[rows, vsl[h]], kd[:, ksl[h]], _TN, preferred_element_type=F32)
                    for h in heads_r])
    states = [st_ref[h] for h in heads_r]
    for c in range(chunks):
        rows = slice(c * CHUNK, (c + 1) * CHUNK)
        for h in heads_r:
            o = intra[c][h] + lax.dot_general(qes[c][:, ksl[h]], states[h].astype(BF16), _NT,
                                              preferred_element_type=F32)
            states[h] = states[h] * decs[c][:, ksl[h]] + upd[c][h]
            o = o * lax.rsqrt(jnp.mean(o * o, axis=-1, keepdims=True) + NORM_EPS) * ng
            o_ref[rows, vsl[h]] = (o * _silu(r_ref[rows, vsl[h]].astype(F32))).astype(o_ref.dtype)
    for h in heads_r:
        st_ref[h] = states[h]


def _gla_branch(proj, small, cols, bsz, seq, wa2_pad, ba2, norm_g, tile):
    hdk = wa2_pad.shape[1]
    dv = norm_g.shape[1]
    heads = 4
    dk = hdk // heads
    hdv = heads * dv
    nt = seq // tile
    full = lambda b, t: (0, 0)
    qcol, kcol = cols["gla_q"] // hdk, cols["gla_k"] // hdk
    vcol, rcol = cols["gla_v"] // hdv, cols["gla_r"] // hdv
    return pl.pallas_call(
        functools.partial(_gla_kernel, tile=tile, heads=heads, dk=dk, dv=dv),
        grid=(bsz, nt),
        in_specs=[pl.BlockSpec((tile, hdk), lambda b, t: (b * nt + t, qcol)),
                  pl.BlockSpec((tile, hdk), lambda b, t: (b * nt + t, kcol)),
                  pl.BlockSpec((tile, hdv), lambda b, t: (b * nt + t, vcol)),
                  pl.BlockSpec((tile, hdv), lambda b, t: (b * nt + t, rcol)),
                  pl.BlockSpec((tile, LANES), lambda b, t: (b * nt + t, 0)),
                  pl.BlockSpec((LANES, hdk), full),
                  pl.BlockSpec((1, hdk), full),
                  pl.BlockSpec((1, dv), full)],
        out_specs=pl.BlockSpec((tile, hdv), lambda b, t: (b * nt + t, 0)),
        out_shape=jax.ShapeDtypeStruct((bsz * seq, hdv), BF16),
        scratch_shapes=[pltpu.VMEM((heads, dv, dk), F32)],
        compiler_params=_params("parallel", "arbitrary"),
        name="gla",
    )(proj["gla_q"], proj["gla_k"], proj["gla_v"], proj["gla_r"], small, wa2_pad, ba2, norm_g)


def _pair_blockdiag(y, lo):
    return jnp.concatenate([jnp.where(lo, y, 0.0), jnp.where(lo, 0.0, y)], axis=0).astype(BF16)


def _unit_lower_inverse_minus_eye(mats, row, col, lo):
    def mul(xs, ys):
        return [jnp.dot(x.astype(BF16), _pair_blockdiag(y, lo), preferred_element_type=F32)
                for x, y in zip(xs, ys)]

    eye = (row == col).astype(F32)
    blk16 = (row // 16) == (col // 16)
    blk32 = (row // 32) == (col // 32)
    off16 = jnp.logical_and(blk32, jnp.logical_not(blk16))
    d = [jnp.where(blk16, a, 0.0) for a in mats]
    d2 = mul(d, d)
    d4 = mul(d2, d2)
    d8 = mul(d4, d4)
    t = mul([eye - x for x in d], [eye + x for x in d2])
    t = mul(t, [eye + x for x in d4])
    t = mul(t, [eye + x for x in d8])
    c1 = [jnp.where(off16, a, 0.0) for a in mats]
    u = mul(mul(t, c1), t)
    t = [x - y for x, y in zip(t, u)]
    c2 = [jnp.where(blk32, 0.0, a) for a in mats]
    u = mul(mul(t, c2), t)
    return [x - y - eye for x, y in zip(t, u)]


def _dn_kernel(qkv_ref, z_ref, sm_ref, cw_ref, alog_ref, dtb_ref, ng_ref, o_ref,
               xbuf, act, gsc, bsc, val_s, kc_s, qg_s, kg_s, aq_s, st_ref,
               *, tile, heads, dk, dv, b_lane, a_lane, group):
    t = pl.program_id(1)
    slabs = xbuf.shape[0]

    @pl.when(t == 0)
    def _():
        st_ref[...] = jnp.zeros(st_ref.shape, F32)
        for s in range(slabs):
            xbuf[s, 0:SUBLANES, :] = jnp.zeros((SUBLANES, LANES), F32)

    @pl.when(t > 0)
    def _():
        for s in range(slabs):
            xbuf[s, 0:SUBLANES, :] = xbuf[s, tile:tile + SUBLANES, :]

    for s in range(slabs):
        xbuf[s, SUBLANES:SUBLANES + tile, :] = qkv_ref[:, s * LANES:(s + 1) * LANES].astype(F32)

    taps = CONV_WIDTH - 1
    for s in range(slabs):
        ls = slice(s * LANES, (s + 1) * LANES)
        for u in range(tile // UNIT):
            x = {j: xbuf[s, pl.ds(SUBLANES + u * UNIT + j, SUBLANES, stride=SUBLANES), :]
                 for j in range(-taps, SUBLANES)}
            regs = []
            for j in range(SUBLANES):
                acc = cw_ref[taps:taps + 1, ls] * x[j]
                for k in range(taps):
                    acc = acc + cw_ref[k:k + 1, ls] * x[j - taps + k]
                regs.append(acc)
            blk = _silu(jnp.concatenate(regs, axis=0))
            if s < 2 * heads:
                scale = dk ** -0.5 if s < heads else 1.0
                blk = blk * (lax.rsqrt(jnp.sum(blk * blk, axis=-1, keepdims=True) + NORM_EPS) * scale)
            for j in range(SUBLANES):
                act[s, pl.ds(u * UNIT + j, SUBLANES, stride=SUBLANES), :] = blk[j * SUBLANES:(j + 1) * SUBLANES]

    sm = sm_ref[...]
    bsc[...] = _sigmoid(sm)
    gsc[...] = _group_cumsum(-jnp.exp(alog_ref[...]) * _softplus(sm + dtb_ref[...]), CHUNK)

    row = lax.broadcasted_iota(jnp.int32, (CHUNK, LANES), 0)
    lane = lax.broadcasted_iota(jnp.int32, (CHUNK, LANES), 1)
    col = lane & (CHUNK - 1)
    lo = lane < CHUNK
    lo_row = lo[0:1]
    incl = row >= col
    strict = row > col
    first = lax.broadcasted_iota(jnp.int32, (CHUNK, 2 * dk), 1) < dk
    ng = ng_ref[...]
    pairs = heads // 2

    def pick(arr, l0, wide):
        return jnp.where(first if wide else lo, arr[:, l0:l0 + 1], arr[:, l0 + 1:l0 + 2])

    def prepare(gidx, carry):
        low, rhs2 = [], []
        for cc in range(group):
            r0 = (gidx * group + cc) * CHUNK
            rows = pl.ds(r0, CHUNK)
            gc = gsc[rows, :]
            be = bsc[rows, :]
            eg = jnp.exp(gc)
            egl = jnp.exp(gc[CHUNK - 1:CHUNK, :] - gc)
            gct = jnp.concatenate([gc, gc], axis=0).T
            for p in range(pairs):
                h0 = 2 * p
                pair = lambda base: jnp.concatenate([act[base + h0, rows, :], act[base + h0 + 1, rows, :]], axis=1)
                q2, k2, v2 = pair(0), pair(heads), pair(2 * heads)
                bcol = pick(be, b_lane + h0, True)
                egc = pick(eg, a_lane + h0, True)
                kb2 = k2 * bcol
                lhs = jnp.concatenate([kb2, q2], axis=0).astype(BF16)
                kbd = jnp.concatenate([jnp.where(first, k2, 0.0), jnp.where(first, 0.0, k2)],
                                      axis=0).astype(BF16)
                both = lax.dot_general(lhs, kbd, _NT, preferred_element_type=F32)
                la = a_lane + h0
                diff = pick(gc, la, False) - jnp.where(lo_row, gct[la:la + 1, :], gct[la + 1:la + 2, :])
                decay = jnp.where(incl, jnp.exp(jnp.where(incl, diff, 0.0)), 0.0)
                low.append(jnp.where(strict, both[:CHUNK] * decay, 0.0))
                aqk = both[CHUNK:] * decay
                aq_s[rows, p * 2 * LANES:(p * 2 + 1) * LANES] = jnp.where(lo, aqk, 0.0).astype(BF16)
                aq_s[rows, (p * 2 + 1) * LANES:(p * 2 + 2) * LANES] = jnp.where(lo, 0.0, aqk).astype(BF16)
                vb, kbe = v2 * bcol, kb2 * egc
                rhs2.append((rows, h0, jnp.concatenate(
                    [jnp.concatenate([vb[:, :dv], kbe[:, :dk]], axis=1),
                     jnp.concatenate([vb[:, dv:], kbe[:, dk:]], axis=1)], axis=0)))
                qg_s[rows, h0 * dk:(h0 + 2) * dk] = (q2 * egc).astype(BF16)
                kg_s[rows, h0 * dk:(h0 + 2) * dk] = (k2 * pick(egl, a_lane + h0, True)).astype(BF16)
        nmat = _unit_lower_inverse_minus_eye(low, row, col, lo)
        for n, (rows, h0, rhs) in zip(nmat, rhs2):
            sol = rhs + _dot(_pair_blockdiag(n, lo), rhs.astype(BF16))
            for j in range(2):
                h = h0 + j
                val_s[rows, h * dv:(h + 1) * dv] = sol[j * CHUNK:(j + 1) * CHUNK, :dv]
                kc_s[rows, h * dk:(h + 1) * dk] = sol[j * CHUNK:(j + 1) * CHUNK, dv:].astype(BF16)
        return carry

    for gidx in range(tile // (CHUNK * group)):
        prepare(gidx, 0)

    def recur(c, carry):
        r0 = c * CHUNK
        rows = pl.ds(r0, CHUNK)
        edl = jnp.exp(gsc[rows, :][CHUNK - 1:CHUNK, :])
        states = [st_ref[h] for h in range(heads)]
        both = [_dot(jnp.concatenate([kc_s[rows, h * dk:(h + 1) * dk], qg_s[rows, h * dk:(h + 1) * dk]], axis=0),
                     states[h].astype(BF16)) for h in range(heads)]
        vnb = [(val_s[rows, h * dv:(h + 1) * dv] - both[h][:CHUNK]).astype(BF16) for h in range(heads)]
        intra = [_dot(jnp.concatenate([aq_s[rows, p * 2 * LANES:(p * 2 + 1) * LANES],
                                       aq_s[rows, (p * 2 + 1) * LANES:(p * 2 + 2) * LANES]], axis=0),
                      jnp.concatenate([vnb[2 * p], vnb[2 * p + 1]], axis=0)) for p in range(pairs)]
        for h in range(heads):
            la = a_lane + h
            st_ref[h] = states[h] * edl[:, la:la + 1] + lax.dot_general(
                kg_s[rows, h * dk:(h + 1) * dk], vnb[h], _TN, preferred_element_type=F32)
        for h in range(heads):
            j = h % 2
            o = both[h][CHUNK:] + intra[h // 2][j * CHUNK:(j + 1) * CHUNK]
            o = o * lax.rsqrt(jnp.mean(o * o, axis=-1, keepdims=True) + NORM_EPS) * ng
            zc = z_ref[rows, h * dv:(h + 1) * dv].astype(F32)
            o_ref[rows, h * dv:(h + 1) * dv] = (o * _silu(zc)).astype(o_ref.dtype)
        return carry

    for c in range(tile // CHUNK):
        recur(c, 0)


def _dn_branch(proj, small, cols, bsz, seq, conv_w, alog_pad, dtb_pad, norm_g, heads, b_lane, a_lane, tile):
    cq = conv_w.shape[1]
    dv = norm_g.shape[1]
    dk = (cq // heads - dv) // 2
    assert dk == dv == LANES and 2 * CHUNK == LANES and heads % 2 == 0
    hdv = heads * dv
    nt = seq // tile
    full = lambda b, t: (0, 0)
    qcol, zcol = cols["dn_qkv"] // cq, cols["dn_z"] // hdv
    group = tile // CHUNK
    return pl.pallas_call(
        functools.partial(_dn_kernel, tile=tile, heads=heads, dk=dk, dv=dv, b_lane=b_lane, a_lane=a_lane,
                          group=group),
        grid=(bsz, nt),
        in_specs=[pl.BlockSpec((tile, cq), lambda b, t: (b * nt + t, qcol)),
                  pl.BlockSpec((tile, hdv), lambda b, t: (b * nt + t, zcol)),
                  pl.BlockSpec((tile, LANES), lambda b, t: (b * nt + t, 0)),
                  pl.BlockSpec((CONV_WIDTH, cq), full),
                  pl.BlockSpec((1, LANES), full),
                  pl.BlockSpec((1, LANES), full),
                  pl.BlockSpec((1, dv), full)],
        out_specs=pl.BlockSpec((tile, hdv), lambda b, t: (b * nt + t, 0)),
        out_shape=jax.ShapeDtypeStruct((bsz * seq, hdv), BF16),
        scratch_shapes=[pltpu.VMEM((cq // LANES, tile + SUBLANES, LANES), F32),
                        pltpu.VMEM((cq // LANES, tile, LANES), F32),
                        pltpu.VMEM((tile, LANES), F32),
                        pltpu.VMEM((tile, LANES), F32),
                        pltpu.VMEM((tile, hdv), F32),
                        pltpu.VMEM((tile, heads * dk), BF16),
                        pltpu.VMEM((tile, heads * dk), BF16),
                        pltpu.VMEM((tile, heads * dk), BF16),
                        pltpu.VMEM((tile, heads * LANES), BF16),
                        pltpu.VMEM((heads, dk, dv), F32)],
        compiler_params=_params("parallel", "arbitrary"),
        name="gated_deltanet",
    )(proj["dn_qkv"], proj["dn_z"], small, conv_w, alog_pad, dtb_pad, norm_g)


def _merge_kernel(y1_ref, y2_ref, y3_ref, gl_ref, bg_ref, wb_ref, o_ref):
    d = o_ref.shape[1]
    rb = o_ref.shape[0] // ROW_BLOCKS
    for h in range(ROW_BLOCKS):
        rows = slice(h * rb, (h + 1) * rb)
        acc = None
        for i, y_ref in enumerate((y1_ref, y2_ref, y3_ref)):
            gate = _sigmoid(gl_ref[rows, i * d:(i + 1) * d].astype(F32) + bg_ref[:, i * d:(i + 1) * d])
            term = gate * _dot(y_ref[rows, :], wb_ref[i])
            acc = term if acc is None else acc + term
        o_ref[rows, :] = acc.astype(o_ref.dtype)


def _merge(ys, proj, cols, b_gate, w_branch, tm):
    m, c = ys[0].shape
    nb, _, d = w_branch.shape
    gcol = cols["gate"] // (nb * d)
    yspec = pl.BlockSpec((tm, c), lambda i: (i, 0))
    return pl.pallas_call(
        _merge_kernel,
        grid=(m // tm,),
        in_specs=[yspec, yspec, yspec,
                  pl.BlockSpec((tm, nb * d), lambda i: (i, gcol)),
                  pl.BlockSpec((1, nb * d), lambda i: (0, 0)),
                  pl.BlockSpec((nb, c, d), lambda i: (0, 0, 0), pipeline_mode=pl.Buffered(1))],
        out_specs=pl.BlockSpec((tm, d), lambda i: (i, 0)),
        out_shape=jax.ShapeDtypeStruct((m, d), BF16),
        compiler_params=_params("parallel"),
        name="branch_merge",
    )(*ys, proj["gate"], b_gate, w_branch)


def _layernorm_store(y, g_ref, b_ref, of_ref, ob_ref, rows=slice(None)):
    mu = jnp.mean(y, axis=-1, keepdims=True)
    yc = y - mu
    var = jnp.mean(yc * yc, axis=-1, keepdims=True)
    out = yc * lax.rsqrt(var + LN_EPS) * g_ref[...] + b_ref[...]
    of_ref[rows, :] = out
    ob_ref[rows, :] = out.astype(ob_ref.dtype)


def _outln_kernel(m_ref, w_ref, x_ref, g_ref, b_ref, of_ref, ob_ref, *, alpha):
    rb = x_ref.shape[0] // ROW_BLOCKS
    for h in range(ROW_BLOCKS):
        rows = slice(h * rb, (h + 1) * rb)
        y = alpha * x_ref[rows, :] + _dot(m_ref[rows, :], w_ref[...])
        _layernorm_store(y, g_ref, b_ref, of_ref, ob_ref, rows)


def _out_ln(merged, w_out, x, g, b, alpha, tm):
    m, d = x.shape
    row = pl.BlockSpec((tm, d), lambda i: (i, 0))
    vec = pl.BlockSpec((1, d), lambda i: (0, 0))
    return pl.pallas_call(
        functools.partial(_outln_kernel, alpha=alpha),
        grid=(m // tm,),
        in_specs=[row, pl.BlockSpec((d, d), lambda i: (0, 0), pipeline_mode=pl.Buffered(1)), row, vec, vec],
        out_specs=[row, row],
        out_shape=[jax.ShapeDtypeStruct((m, d), F32), jax.ShapeDtypeStruct((m, d), BF16)],
        compiler_params=_params("parallel"),
        name="out_proj_ln",
    )(merged, w_out, x, g, b)


def _mlp_kernel(xb_ref, xf_ref, w1_ref, b1_ref, w2_ref, b2_ref, g_ref, b_ref, of_ref, ob_ref, acc_ref,
                *, alpha):
    f = pl.program_id(1)

    @pl.when(f == 0)
    def _():
        acc_ref[...] = jnp.zeros(acc_ref.shape, F32)

    rb = xb_ref.shape[0] // ROW_BLOCKS
    hidden = []
    for h in range(ROW_BLOCKS):
        a = jnp.maximum(_dot(xb_ref[h * rb:(h + 1) * rb, :], w1_ref[...]) + b1_ref[...], 0.0)
        hidden.append((a * a).astype(BF16))
    for h in range(ROW_BLOCKS):
        acc_ref[h * rb:(h + 1) * rb, :] += _dot(hidden[h], w2_ref[...])

    @pl.when(f == pl.num_programs(1) - 1)
    def _():
        y = alpha * xf_ref[...] + (acc_ref[...] + b2_ref[...])
        _layernorm_store(y, g_ref, b_ref, of_ref, ob_ref)


def _mlp_ln(xb, xf, w1, b1, w2, b2, g, b, alpha, tm, tf):
    m, d = xf.shape
    ff = w1.shape[1]
    row = pl.BlockSpec((tm, d), lambda i, f: (i, 0))
    vec = pl.BlockSpec((1, d), lambda i, f: (0, 0))
    return pl.pallas_call(
        functools.partial(_mlp_kernel, alpha=alpha),
        grid=(m // tm, ff // tf),
        in_specs=[row, row,
                  pl.BlockSpec((d, tf), lambda i, f: (0, f)),
                  pl.BlockSpec((1, tf), lambda i, f: (0, f)),
                  pl.BlockSpec((tf, d), lambda i, f: (f, 0)),
                  vec, vec, vec],
        out_specs=[row, row],
        out_shape=[jax.ShapeDtypeStruct((m, d), F32), jax.ShapeDtypeStruct((m, d), BF16)],
        scratch_shapes=[pltpu.VMEM((tm, d), F32)],
        compiler_params=_params("parallel", "arbitrary"),
        name="mlp_ln",
    )(xb, xf, w1, b1, w2, b2, g, b)


def _projection_groups(names, sizes, src):
    groups = []
    for name in names:
        if sizes[name] < LANES:
            continue
        if groups:
            start, width, members = groups[-1]
            if src[name] == start + width and width % sizes[name] == 0:
                members[name] = width
                groups[-1] = (start, width + sizes[name], members)
                continue
        groups.append((src[name], sizes[name], {name: 0}))
    return groups


def _group_block_diag(w, width=2 * LANES):
    l, g, n, _ = w.shape
    per = width // n
    eye = jnp.eye(per, dtype=w.dtype)
    wg = w.reshape(l, g // per, per, n, n)
    return (eye[None, None, :, None, :, None] * wg[:, :, :, :, None, :]).reshape(l, g // per, width, width)


def kernel(x, w_in, lru_conv_w, lru_conv_b, lru_wa, lru_ba, lru_wi, lru_bi, lru_lambda, gla_wa2, gla_ba2, gla_norm_g, dn_conv_w, dn_a_log, dn_dt_bias, dn_norm_g, w_branch, b_gate, w_out, ln1_g, ln1_b, mlp_w1, mlp_b1, mlp_w2, mlp_b2, ln2_g, ln2_b):
    bsz, seq, d = x.shape
    depth = w_in.shape[0]
    m = bsz * seq
    alpha = (2.0 * depth) ** 0.25

    lru_w = lru_conv_w.shape[-1]
    gla_heads = 4
    gla_rank, gla_hdk = gla_wa2.shape[1], gla_wa2.shape[2]
    gla_hdv = gla_heads * gla_norm_g.shape[-1]
    dn_heads = dn_a_log.shape[-1]
    dn_cq = dn_conv_w.shape[-1]
    dn_hdv = dn_heads * dn_norm_g.shape[-1]
    nb = w_branch.shape[1]
    sizes = {"lru_x": lru_w, "lru_y": lru_w, "gla_q": gla_hdk, "gla_k": gla_hdk, "gla_v": gla_hdv,
             "gla_alr": gla_rank, "gla_r": gla_hdv, "dn_qkv": dn_cq, "dn_b": dn_heads, "dn_a": dn_heads,
             "dn_z": dn_hdv, "gate": nb * d}
    ref_order = ("lru_x", "lru_y", "gla_q", "gla_k", "gla_v", "gla_alr", "gla_r",
                 "dn_qkv", "dn_b", "dn_a", "dn_z", "gate")
    src, off = {}, 0
    for name in ref_order:
        src[name] = off
        off += sizes[name]
    assert off == w_in.shape[-1]
    take = lambda name: w_in[:, :, src[name]:src[name] + sizes[name]]
    groups = _projection_groups(ref_order, sizes, src)
    w_groups = [w_in[:, :, start:start + width].astype(BF16) for start, width, _ in groups]
    cols = {name: off for _, _, members in groups for name, off in members.items()}
    b_lane, a_lane = gla_rank, gla_rank + dn_heads
    n_small = gla_rank + 2 * dn_heads
    w_small = jnp.concatenate([take("gla_alr"), take("dn_b"), take("dn_a"),
                               jnp.zeros((depth, d, LANES - n_small), w_in.dtype)], axis=-1).astype(BF16)
    wa2_pad = jnp.concatenate([gla_wa2, jnp.zeros((depth, LANES - gla_rank, gla_hdk), gla_wa2.dtype)],
                              axis=1).astype(BF16)

    def lane_pad(p, lane):
        return jnp.pad(p, ((0, 0), (lane, LANES - lane - p.shape[1])))[:, None, :]

    alog_pad = lane_pad(dn_a_log, a_lane)
    dtb_pad = lane_pad(dn_dt_bias, a_lane)
    wa_grp = _group_block_diag(lru_wa).astype(BF16)
    wi_grp = _group_block_diag(lru_wi).astype(BF16)
    wb16, wo16 = w_branch.astype(BF16), w_out.astype(BF16)
    w1_16, w2_16 = mlp_w1.astype(BF16), mlp_w2.astype(BF16)
    vec = lambda p, l: p[l][None, :]

    tile = min(256, seq)
    tile_wide = min(512, seq)
    tm_proj = min(2048, m)
    xf = x.reshape(m, d)
    xb = xf.astype(BF16)
    for l in range(depth):
        proj = {}
        for (_, width, members), wg in zip(groups, w_groups):
            out = _matmul(xb, wg, l, BF16, tm_proj, min(1024, width), "in_proj")
            proj.update({name: out for name in members})
        small = _matmul(xb, w_small, l, F32, tm_proj, LANES, "in_proj_small")
        y_lru = _lru_branch(proj, cols, bsz, seq, lru_conv_w[l], vec(lru_conv_b, l), wa_grp[l],
                            vec(lru_ba, l), wi_grp[l], vec(lru_bi, l), vec(lru_lambda, l), tile_wide)
        y_gla = _gla_branch(proj, small, cols, bsz, seq, wa2_pad[l], vec(gla_ba2, l), vec(gla_norm_g, l), tile_wide)
        y_dn = _dn_branch(proj, small, cols, bsz, seq, dn_conv_w[l], alog_pad[l], dtb_pad[l],
                          vec(dn_norm_g, l), dn_heads, b_lane, a_lane, tile)
        merged = _merge((y_lru, y_gla, y_dn), proj, cols, b_gate[l].reshape(1, nb * d), wb16[l], min(512, m))
        xf, xb = _out_ln(merged, wo16[l], xf, vec(ln1_g, l), vec(ln1_b, l), alpha, min(512, m))
        xf, xb = _mlp_ln(xb, xf, w1_16[l], vec(mlp_b1, l), w2_16[l], vec(mlp_b2, l),
                         vec(ln2_g, l), vec(ln2_b, l), alpha, min(512, m), 1024)
    return xf.reshape(bsz, seq, d)
```

```python
import functools

import jax
import jax.numpy as jnp
from jax import lax
from jax.experimental import pallas as pl
from jax.experimental.pallas import tpu as pltpu

F32 = jnp.float32
BF16 = jnp.bfloat16

CONV_WIDTH = 4
CHUNK = 64
LRU_C = 8.0
GLA_TAU = 16.0
LN_EPS = 1e-5
NORM_EPS = 1e-6

SUBLANES = 8
LANES = 128
VMEM_LIMIT_BYTES = 56 * 1024 * 1024
ROW_BLOCKS = 2

_NT = (((1,), (1,)), ((), ()))
_TN = (((0,), (0,)), ((), ()))


def _params(*sem):
    return pltpu.CompilerParams(dimension_semantics=sem, vmem_limit_bytes=VMEM_LIMIT_BYTES)


def _sigmoid(x):
    return 0.5 * (1.0 + jnp.tanh(0.5 * x))


def _softplus(x):
    return jnp.maximum(x, 0.0) + jnp.log1p(jnp.exp(-jnp.abs(x)))


def _silu(x):
    h = 0.5 * x
    return h + h * jnp.tanh(h)


def _gelu_tanh(x):
    return x * (0.5 * (1.0 + jnp.tanh(0.7978845608028654 * (x + 0.044715 * (x * x * x)))))


def _dot(a, b):
    return jnp.dot(a, b, preferred_element_type=F32)


def _group_cumsum(x, period):
    pos = lax.broadcasted_iota(jnp.int32, x.shape, 0) & (period - 1)
    k = 1
    while k < period:
        x = x + jnp.where(pos >= k, pltpu.roll(x, k, 0), 0.0)
        k *= 2
    return x


def _mm_kernel(x_ref, w_ref, o_ref):
    o_ref[...] = _dot(x_ref[...], w_ref[...]).astype(o_ref.dtype)


def _matmul(x, w, layer, out_dtype, tm, tn, name):
    m, k = x.shape
    n = w.shape[2]
    assert m % tm == 0 and n % tn == 0
    return pl.pallas_call(
        _mm_kernel,
        grid=(n // tn, m // tm),
        in_specs=[pl.BlockSpec((tm, k), lambda j, i: (i, 0)),
                  pl.BlockSpec((None, k, tn), lambda j, i: (layer, 0, j))],
        out_specs=pl.BlockSpec((tm, tn), lambda j, i: (i, j)),
        out_shape=jax.ShapeDtypeStruct((m, n), out_dtype),
        compiler_params=_params("parallel", "parallel"),
        name=name,
    )(x, w)


UNIT = SUBLANES * SUBLANES


def _lru_kernel(x_ref, y_ref, cw_ref, cb_ref, wa_ref, ba_ref, wi_ref, bi_ref, lam_ref,
                o_ref, xs, hs, hcar, *, tile):
    t = pl.program_id(1)
    slabs = xs.shape[0]
    c = slabs * LANES
    groups = wa_ref.shape[0]
    gw = c // groups

    @pl.when(t == 0)
    def _():
        hcar[...] = jnp.zeros(hcar.shape, F32)
        for s in range(slabs):
            xs[s, 0:SUBLANES, :] = jnp.zeros((SUBLANES, LANES), F32)

    @pl.when(t > 0)
    def _():
        for s in range(slabs):
            xs[s, 0:SUBLANES, :] = xs[s, tile:tile + SUBLANES, :]

    for s in range(slabs):
        xs[s, SUBLANES:SUBLANES + tile, :] = x_ref[:, s * LANES:(s + 1) * LANES].astype(F32)

    units = tile // UNIT
    taps = CONV_WIDTH - 1
    xc_units = []
    for u in range(units):
        per_slab = []
        for s in range(slabs):
            ls = slice(s * LANES, (s + 1) * LANES)
            x = {j: xs[s, pl.ds(SUBLANES + u * UNIT + j, SUBLANES, stride=SUBLANES), :]
                 for j in range(-taps, SUBLANES)}
            regs = []
            for j in range(SUBLANES):
                acc = cb_ref[:, ls] + cw_ref[taps:taps + 1, ls] * x[j]
                for k in range(taps):
                    acc = acc + cw_ref[k:k + 1, ls] * x[j - taps + k]
                regs.append(acc)
            per_slab.append(jnp.concatenate(regs, axis=0))
        xc_units.append(jnp.concatenate(per_slab, axis=1))
    xc = jnp.concatenate(xc_units, axis=0)
    xcb = xc.astype(BF16)

    def gate(w_ref, b_ref):
        pre = jnp.concatenate([_dot(xcb[:, g * gw:(g + 1) * gw], w_ref[g]) for g in range(groups)], axis=1)
        return _sigmoid(pre + b_ref[...])

    r = gate(wa_ref, ba_ref)
    i = gate(wi_ref, bi_ref)
    log_a = (-LRU_C * _softplus(-lam_ref[...])) * r
    a = jnp.exp(log_a)
    v = jnp.sqrt(-jnp.tanh(log_a) * (a * a + 1.0)) * (i * xc)

    q = lax.broadcasted_iota(jnp.int32, (SUBLANES, c), 0)
    cin = hcar[0:1, :]
    for u in range(units):
        reg = lambda arr, j: arr[u * UNIT + j * SUBLANES:u * UNIT + (j + 1) * SUBLANES, :]
        h, p = [reg(v, 0)], [reg(a, 0)]
        for j in range(1, SUBLANES):
            h.append(reg(a, j) * h[-1] + reg(v, j))
            p.append(reg(a, j) * p[-1])
        hc, pc = h[-1], p[-1]
        k = 1
        while k < SUBLANES:
            m = q >= k
            hc = jnp.where(m, pc * pltpu.roll(hc, k, 0) + hc, hc)
            pc = jnp.where(m, pc * pltpu.roll(pc, k, 0), pc)
            k *= 2
        end = hc + pc * cin
        carry = jnp.where(q == 0, cin, pltpu.roll(end, 1, 0))
        cin = end[SUBLANES - 1:SUBLANES, :]
        for j in range(SUBLANES):
            full = h[j] + p[j] * carry
            for s in range(slabs):
                hs[s, pl.ds(u * UNIT + j, SUBLANES, stride=SUBLANES), :] = full[:, s * LANES:(s + 1) * LANES]
    hcar[...] = jnp.broadcast_to(cin, hcar.shape)

    for s in range(slabs):
        ls = slice(s * LANES, (s + 1) * LANES)
        o_ref[:, ls] = (hs[s] * _gelu_tanh(y_ref[:, ls].astype(F32))).astype(o_ref.dtype)


def _lru_branch(proj, cols, bsz, seq, conv_w, conv_b, wa_grp, ba, wi_grp, bi, lam, tile):
    c = conv_w.shape[1]
    assert c % LANES == 0 and tile % UNIT == 0
    nt = seq // tile
    xcol, ycol = cols["lru_x"] // c, cols["lru_y"] // c
    full = lambda b, t: (0, 0)
    wspec = pl.BlockSpec(wa_grp.shape, lambda b, t: (0, 0, 0))
    return pl.pallas_call(
        functools.partial(_lru_kernel, tile=tile),
        grid=(bsz, nt),
        in_specs=[pl.BlockSpec((tile, c), lambda b, t: (b * nt + t, xcol)),
                  pl.BlockSpec((tile, c), lambda b, t: (b * nt + t, ycol)),
                  pl.BlockSpec((CONV_WIDTH, c), full),
                  pl.BlockSpec((1, c), full),
                  wspec,
                  pl.BlockSpec((1, c), full),
                  wspec,
                  pl.BlockSpec((1, c), full),
                  pl.BlockSpec((1, c), full)],
        out_specs=pl.BlockSpec((tile, c), lambda b, t: (b * nt + t, 0)),
        out_shape=jax.ShapeDtypeStruct((bsz * seq, c), BF16),
        scratch_shapes=[pltpu.VMEM((c // LANES, tile + SUBLANES, LANES), F32),
                        pltpu.VMEM((c // LANES, tile, LANES), F32),
                        pltpu.VMEM((SUBLANES, c), F32)],
        compiler_params=_params("parallel", "arbitrary"),
        name="rg_lru",
    )(proj["lru_x"], proj["lru_y"], conv_w, conv_b, wa_grp, ba, wi_grp, bi, lam)


def _gla_kernel(q_ref, k_ref, v_ref, r_ref, sm_ref, wa2_ref, ba2_ref, ng_ref, o_ref, st_ref,
                *, tile, heads, dk, dv):
    t = pl.program_id(1)

    @pl.when(t == 0)
    def _():
        st_ref[...] = jnp.zeros(st_ref.shape, F32)

    z = _dot(sm_ref[...].astype(BF16), wa2_ref[...]) + ba2_ref[...]
    gk = (jnp.minimum(z, 0.0) - jnp.log1p(jnp.exp(-jnp.abs(z)))) / GLA_TAU
    b = _group_cumsum(gk, CHUNK)
    row = lax.broadcasted_iota(jnp.int32, (CHUNK, CHUNK), 0)
    col = lax.broadcasted_iota(jnp.int32, (CHUNK, CHUNK), 1)
    causal = row >= col
    scale = dk ** -0.5
    ng = ng_ref[...]
    chunks = tile // CHUNK
    heads_r = range(heads)
    ksl = [slice(h * dk, (h + 1) * dk) for h in heads_r]
    vsl = [slice(h * dv, (h + 1) * dv) for h in heads_r]
    qes, decs, intra, upd = [], [], [], []
    for c in range(chunks):
        rows = slice(c * CHUNK, (c + 1) * CHUNK)
        bc = b[rows]
        bl = bc[CHUNK - 1:CHUNK]
        qf = q_ref[rows, :].astype(F32)
        kf = k_ref[rows, :].astype(F32)
        qe = ((qf * scale) * jnp.exp(bc)).astype(BF16)
        ke = (kf * jnp.exp(-bc)).astype(BF16)
        kd = (kf * jnp.exp(bl - bc)).astype(BF16)
        qes.append(qe)
        decs.append(jnp.exp(bl))
        att = [jnp.where(causal, lax.dot_general(qe[:, ksl[h]], ke[:, ksl[h]], _NT, preferred_element_type=F32),
                         0.0).astype(BF16) for h in heads_r]
        intra.append([_dot(att[h], v_ref[rows, vsl[h]]) for h in heads_r])
        upd.append([lax.dot_general(v_ref[rows, vsl[h]], kd[:, ksl[h]], _TN, preferred_element_type=F32)
                    for h in heads_r])
    states = [st_ref[h] for h in heads_r]
    for c in range(chunks):
        rows = slice(c * CHUNK, (c + 1) * CHUNK)
        for h in heads_r:
            o = intra[c][h] + lax.dot_general(qes[c][:, ksl[h]], states[h].astype(BF16), _NT,
                                              preferred_element_type=F32)
            states[h] = states[h] * decs[c][:, ksl[h]] + upd[c][h]
            o = o * lax.rsqrt(jnp.mean(o * o, axis=-1, keepdims=True) + NORM_EPS) * ng
            o_ref[rows, vsl[h]] = (o * _silu(r_ref[rows, vsl[h]].astype(F32))).astype(o_ref.dtype)
    for h in heads_r:
        st_ref[h] = states[h]


def _gla_branch(proj, small, cols, bsz, seq, wa2_pad, ba2, norm_g, tile):
    hdk = wa2_pad.shape[1]
    dv = norm_g.shape[1]
    heads = 4
    dk = hdk // heads
    hdv = heads * dv
    nt = seq // tile
    full = lambda b, t: (0, 0)
    qcol, kcol = cols["gla_q"] // hdk, cols["gla_k"] // hdk
    vcol, rcol = cols["gla_v"] // hdv, cols["gla_r"] // hdv
    return pl.pallas_call(
        functools.partial(_gla_kernel, tile=tile, heads=heads, dk=dk, dv=dv),
        grid=(bsz, nt),
        in_specs=[pl.BlockSpec((tile, hdk), lambda b, t: (b * nt + t, qcol)),
                  pl.BlockSpec((tile, hdk), lambda b, t: (b * nt + t, kcol)),
                  pl.BlockSpec((tile, hdv), lambda b, t: (b * nt + t, vcol)),
                  pl.BlockSpec((tile, hdv), lambda b, t: (b * nt + t, rcol)),
                  pl.BlockSpec((tile, LANES), lambda b, t: (b * nt + t, 0)),
                  pl.BlockSpec((LANES, hdk), full),
                  pl.BlockSpec((1, hdk), full),
                  pl.BlockSpec((1, dv), full)],
        out_specs=pl.BlockSpec((tile, hdv), lambda b, t: (b * nt + t, 0)),
        out_shape=jax.ShapeDtypeStruct((bsz * seq, hdv), BF16),
        scratch_shapes=[pltpu.VMEM((heads, dv, dk), F32)],
        compiler_params=_params("parallel", "arbitrary"),
        name="gla",
    )(proj["gla_q"], proj["gla_k"], proj["gla_v"], proj["gla_r"], small, wa2_pad, ba2, norm_g)


def _pair_blockdiag(y, lo):
    return jnp.concatenate([jnp.where(lo, y, 0.0), jnp.where(lo, 0.0, y)], axis=0).astype(BF16)


def _unit_lower_inverse_minus_eye(mats, row, col, lo):
    def mul(xs, ys):
        return [jnp.dot(x.astype(BF16), _pair_blockdiag(y, lo), preferred_element_type=F32)
                for x, y in zip(xs, ys)]

    eye = (row == col).astype(F32)
    blk16 = (row // 16) == (col // 16)
    blk32 = (row // 32) == (col // 32)
    off16 = jnp.logical_and(blk32, jnp.logical_not(blk16))
    d = [jnp.where(blk16, a, 0.0) for a in mats]
    d2 = mul(d, d)
    d4 = mul(d2, d2)
    d8 = mul(d4, d4)
    t = mul([eye - x for x in d], [eye + x for x in d2])
    t = mul(t, [eye + x for x in d4])
    t = mul(t, [eye + x for x in d8])
    c1 = [jnp.where(off16, a, 0.0) for a in mats]
    u = mul(mul(t, c1), t)
    t = [x - y for x, y in zip(t, u)]
    c2 = [jnp.where(blk32, 0.0, a) for a in mats]
    u = mul(mul(t, c2), t)
    return [x - y - eye for x, y in zip(t, u)]


def _dn_kernel(qkv_ref, z_ref, sm_ref, cw_ref, alog_ref, dtb_ref, ng_ref, o_ref,
               xbuf, act, gsc, bsc, val_s, kc_s, qg_s, kg_s, aq_s, st_ref,
               *, tile, heads, dk, dv, b_lane, a_lane, group):
    t = pl.program_id(1)
    slabs = xbuf.shape[0]

    @pl.when(t == 0)
    def _():
        st_ref[...] = jnp.zeros(st_ref.shape, F32)
        for s in range(slabs):
            xbuf[s, 0:SUBLANES, :] = jnp.zeros((SUBLANES, LANES), F32)

    @pl.when(t > 0)
    def _():
        for s in range(slabs):
            xbuf[s, 0:SUBLANES, :] = xbuf[s, tile:tile + SUBLANES, :]

    for s in range(slabs):
        xbuf[s, SUBLANES:SUBLANES + tile, :] = qkv_ref[:, s * LANES:(s + 1) * LANES].astype(F32)

    taps = CONV_WIDTH - 1
    for s in range(slabs):
        ls = slice(s * LANES, (s + 1) * LANES)
        for u in range(tile // UNIT):
            x = {j: xbuf[s, pl.ds(SUBLANES + u * UNIT + j, SUBLANES, stride=SUBLANES), :]
                 for j in range(-taps, SUBLANES)}
            regs = []
            for j in range(SUBLANES):
                acc = cw_ref[taps:taps + 1, ls] * x[j]
                for k in range(taps):
                    acc = acc + cw_ref[k:k + 1, ls] * x[j - taps + k]
                regs.append(acc)
            blk = _silu(jnp.concatenate(regs, axis=0))
            if s < 2 * heads:
                scale = dk ** -0.5 if s < heads else 1.0
                blk = blk * (lax.rsqrt(jnp.sum(blk * blk, axis=-1, keepdims=True) + NORM_EPS) * scale)
            for j in range(SUBLANES):
                act[s, pl.ds(u * UNIT + j, SUBLANES, stride=SUBLANES), :] = blk[j * SUBLANES:(j + 1) * SUBLANES]

    sm = sm_ref[...]
    bsc[...] = _sigmoid(sm)
    gsc[...] = _group_cumsum(-jnp.exp(alog_ref[...]) * _softplus(sm + dtb_ref[...]), CHUNK)

    row = lax.broadcasted_iota(jnp.int32, (CHUNK, LANES), 0)
    lane = lax.broadcasted_iota(jnp.int32, (CHUNK, LANES), 1)
    col = lane & (CHUNK - 1)
    lo = lane < CHUNK
    lo_row = lo[0:1]
    incl = row >= col
    strict = row > col
    first = lax.broadcasted_iota(jnp.int32, (CHUNK, 2 * dk), 1) < dk
    ng = ng_ref[...]
    pairs = heads // 2

    def pick(arr, l0, wide):
        return jnp.where(first if wide else lo, arr[:, l0:l0 + 1], arr[:, l0 + 1:l0 + 2])

    def prepare(gidx, carry):
        low, rhs2 = [], []
        for cc in range(group):
            r0 = (gidx * group + cc) * CHUNK
            rows = pl.ds(r0, CHUNK)
            gc = gsc[rows, :]
            be = bsc[rows, :]
            eg = jnp.exp(gc)
            egl = jnp.exp(gc[CHUNK - 1:CHUNK, :] - gc)
            gct = jnp.concatenate([gc, gc], axis=0).T
            for p in range(pairs):
                h0 = 2 * p
                pair = lambda base: jnp.concatenate([act[base + h0, rows, :], act[base + h0 + 1, rows, :]], axis=1)
                q2, k2, v2 = pair(0), pair(heads), pair(2 * heads)
                bcol = pick(be, b_lane + h0, True)
                egc = pick(eg, a_lane + h0, True)
                kb2 = k2 * bcol
                lhs = jnp.concatenate([kb2, q2], axis=0).astype(BF16)
                kbd = jnp.concatenate([jnp.where(first, k2, 0.0), jnp.where(first, 0.0, k2)],
                                      axis=0).astype(BF16)
                both = lax.dot_general(lhs, kbd, _NT, preferred_element_type=F32)
                la = a_lane + h0
                diff = pick(gc, la, False) - jnp.where(lo_row, gct[la:la + 1, :], gct[la + 1:la + 2, :])
                decay = jnp.where(incl, jnp.exp(jnp.where(incl, diff, 0.0)), 0.0)
                low.append(jnp.where(strict, both[:CHUNK] * decay, 0.0))
                aqk = both[CHUNK:] * decay
                aq_s[rows, p * 2 * LANES:(p * 2 + 1) * LANES] = jnp.where(lo, aqk, 0.0).astype(BF16)
                aq_s[rows, (p * 2 + 1) * LANES:(p * 2 + 2) * LANES] = jnp.where(lo, 0.0, aqk).astype(BF16)
                vb, kbe = v2 * bcol, kb2 * egc
                rhs2.append((rows, h0, jnp.concatenate(
                    [jnp.concatenate([vb[:, :dv], kbe[:, :dk]], axis=1),
                     jnp.concatenate([vb[:, dv:], kbe[:, dk:]], axis=1)], axis=0)))
                qg_s[rows, h0 * dk:(h0 + 2) * dk] = (q2 * egc).astype(BF16)
                kg_s[rows, h0 * dk:(h0 + 2) * dk] = (k2 * pick(egl, a_lane + h0, True)).astype(BF16)
        nmat = _unit_lower_inverse_minus_eye(low, row, col, lo)
        for n, (rows, h0, rhs) in zip(nmat, rhs2):
            sol = rhs + _dot(_pair_blockdiag(n, lo), rhs.astype(BF16))
            for j in range(2):
                h = h0 + j
                val_s[rows, h * dv:(h + 1) * dv] = sol[j * CHUNK:(j + 1) * CHUNK, :dv]
                kc_s[rows, h * dk:(h + 1) * dk] = sol[j * CHUNK:(j + 1) * CHUNK, dv:].astype(BF16)
        return carry

    for gidx in range(tile // (CHUNK * group)):
        prepare(gidx, 0)

    def recur(c, carry):
        r0 = c * CHUNK
        rows = pl.ds(r0, CHUNK)
        edl = jnp.exp(gsc[rows, :][CHUNK - 1:CHUNK, :])
        states = [st_ref[h] for h in range(heads)]
        both = [_dot(jnp.concatenate([kc_s[rows, h * dk:(h + 1) * dk], qg_s[rows, h * dk:(h + 1) * dk]], axis=0),
                     states[h].astype(BF16)) for h in range(heads)]
        vnb = [(val_s[rows, h * dv:(h + 1) * dv] - both[h][:CHUNK]).astype(BF16) for h in range(heads)]
        intra = [_dot(jnp.concatenate([aq_s[rows, p * 2 * LANES:(p * 2 + 1) * LANES],
                                       aq_s[rows, (p * 2 + 1) * LANES:(p * 2 + 2) * LANES]], axis=0),
                      jnp.concatenate([vnb[2 * p], vnb[2 * p + 1]], axis=0)) for p in range(pairs)]
        for h in range(heads):
            la = a_lane + h
            st_ref[h] = states[h] * edl[:, la:la + 1] + lax.dot_general(
                kg_s[rows, h * dk:(h + 1) * dk], vnb[h], _TN, preferred_element_type=F32)
        for h in range(heads):
            j = h % 2
            o = both[h][CHUNK:] + intra[h // 2][j * CHUNK:(j + 1) * CHUNK]
            o = o * lax.rsqrt(jnp.mean(o * o, axis=-1, keepdims=True) + NORM_EPS) * ng
            zc = z_ref[rows, h * dv:(h + 1) * dv].astype(F32)
            o_ref[rows, h * dv:(h + 1) * dv] = (o * _silu(zc)).astype(o_ref.dtype)
        return carry

    for c in range(tile // CHUNK):
        recur(c, 0)


def _dn_branch(proj, small, cols, bsz, seq, conv_w, alog_pad, dtb_pad, norm_g, heads, b_lane, a_lane, tile):
    cq = conv_w.shape[1]
    dv = norm_g.shape[1]
    dk = (cq // heads - dv) // 2
    assert dk == dv == LANES and 2 * CHUNK == LANES and heads % 2 == 0
    hdv = heads * dv
    nt = seq // tile
    full = lambda b, t: (0, 0)
    qcol, zcol = cols["dn_qkv"] // cq, cols["dn_z"] // hdv
    group = tile // CHUNK
    return pl.pallas_call(
        functools.partial(_dn_kernel, tile=tile, heads=heads, dk=dk, dv=dv, b_lane=b_lane, a_lane=a_lane,
                          group=group),
        grid=(bsz, nt),
        in_specs=[pl.BlockSpec((tile, cq), lambda b, t: (b * nt + t, qcol)),
                  pl.BlockSpec((tile, hdv), lambda b, t: (b * nt + t, zcol)),
                  pl.BlockSpec((tile, LANES), lambda b, t: (b * nt + t, 0)),
                  pl.BlockSpec((CONV_WIDTH, cq), full),
                  pl.BlockSpec((1, LANES), full),
                  pl.BlockSpec((1, LANES), full),
                  pl.BlockSpec((1, dv), full)],
        out_specs=pl.BlockSpec((tile, hdv), lambda b, t: (b * nt + t, 0)),
        out_shape=jax.ShapeDtypeStruct((bsz * seq, hdv), BF16),
        scratch_shapes=[pltpu.VMEM((cq // LANES, tile + SUBLANES, LANES), F32),
                        pltpu.VMEM((cq // LANES, tile, LANES), F32),
                        pltpu.VMEM((tile, LANES), F32),
                        pltpu.VMEM((tile, LANES), F32),
                        pltpu.VMEM((tile, hdv), F32),
                        pltpu.VMEM((tile, heads * dk), BF16),
                        pltpu.VMEM((tile, heads * dk), BF16),
                        pltpu.VMEM((tile, heads * dk), BF16),
                        pltpu.VMEM((tile, heads * LANES), BF16),
                        pltpu.VMEM((heads, dk, dv), F32)],
        compiler_params=_params("parallel", "arbitrary"),
        name="gated_deltanet",
    )(proj["dn_qkv"], proj["dn_z"], small, conv_w, alog_pad, dtb_pad, norm_g)


def _merge_kernel(y1_ref, y2_ref, y3_ref, gl_ref, bg_ref, wb_ref, o_ref):
    d = o_ref.shape[1]
    rb = o_ref.shape[0] // ROW_BLOCKS
    for h in range(ROW_BLOCKS):
        rows = slice(h * rb, (h + 1) * rb)
        acc = None
        for i, y_ref in enumerate((y1_ref, y2_ref, y3_ref)):
            gate = _sigmoid(gl_ref[rows, i * d:(i + 1) * d].astype(F32) + bg_ref[:, i * d:(i + 1) * d])
            term = gate * _dot(y_ref[rows, :], wb_ref[i])
            acc = term if acc is None else acc + term
        o_ref[rows, :] = acc.astype(o_ref.dtype)


def _merge(ys, proj, cols, b_gate, w_branch, tm):
    m, c = ys[0].shape
    nb, _, d = w_branch.shape
    gcol = cols["gate"] // (nb * d)
    yspec = pl.BlockSpec((tm, c), lambda i: (i, 0))
    return pl.pallas_call(
        _merge_kernel,
        grid=(m // tm,),
        in_specs=[yspec, yspec, yspec,
                  pl.BlockSpec((tm, nb * d), lambda i: (i, gcol)),
                  pl.BlockSpec((1, nb * d), lambda i: (0, 0)),
                  pl.BlockSpec((nb, c, d), lambda i: (0, 0, 0), pipeline_mode=pl.Buffered(1))],
        out_specs=pl.BlockSpec((tm, d), lambda i: (i, 0)),
        out_shape=jax.ShapeDtypeStruct((m, d), BF16),
        compiler_params=_params("parallel"),
        name="branch_merge",
    )(*ys, proj["gate"], b_gate, w_branch)


def _layernorm_store(y, g_ref, b_ref, of_ref, ob_ref, rows=slice(None)):
    mu = jnp.mean(y, axis=-1, keepdims=True)
    yc = y - mu
    var = jnp.mean(yc * yc, axis=-1, keepdims=True)
    out = yc * lax.rsqrt(var + LN_EPS) * g_ref[...] + b_ref[...]
    of_ref[rows, :] = out
    ob_ref[rows, :] = out.astype(ob_ref.dtype)


def _outln_kernel(m_ref, w_ref, x_ref, g_ref, b_ref, of_ref, ob_ref, *, alpha):
    rb = x_ref.shape[0] // ROW_BLOCKS
    for h in range(ROW_BLOCKS):
        rows = slice(h * rb, (h + 1) * rb)
        y = alpha * x_ref[rows, :] + _dot(m_ref[rows, :], w_ref[...])
        _layernorm_store(y, g_ref, b_ref, of_ref, ob_ref, rows)


def _out_ln(merged, w_out, x, g, b, alpha, tm):
    m, d = x.shape
    row = pl.BlockSpec((tm, d), lambda i: (i, 0))
    vec = pl.BlockSpec((1, d), lambda i: (0, 0))
    return pl.pallas_call(
        functools.partial(_outln_kernel, alpha=alpha),
        grid=(m // tm,),
        in_specs=[row, pl.BlockSpec((d, d), lambda i: (0, 0), pipeline_mode=pl.Buffered(1)), row, vec, vec],
        out_specs=[row, row],
        out_shape=[jax.ShapeDtypeStruct((m, d), F32), jax.ShapeDtypeStruct((m, d), BF16)],
        compiler_params=_params("parallel"),
        name="out_proj_ln",
    )(merged, w_out, x, g, b)


def _merge_out_ln_kernel(y1_ref, y2_ref, y3_ref, gl_ref, bg_ref, wb_ref, wo_ref, x_ref, g_ref, b_ref,
                         of_ref, ob_ref, *, alpha):
    d = of_ref.shape[1]
    rb = of_ref.shape[0] // ROW_BLOCKS
    merged = []
    for h in range(ROW_BLOCKS):
        rows = slice(h * rb, (h + 1) * rb)
        acc = None
        for i, y_ref in enumerate((y1_ref, y2_ref, y3_ref)):
            gate = _sigmoid(gl_ref[rows, i * d:(i + 1) * d].astype(F32) + bg_ref[:, i * d:(i + 1) * d])
            term = gate * _dot(y_ref[rows, :], wb_ref[i])
            acc = term if acc is None else acc + term
        merged.append(acc.astype(BF16))
    for h in range(ROW_BLOCKS):
        rows = slice(h * rb, (h + 1) * rb)
        y = alpha * x_ref[rows, :] + _dot(merged[h], wo_ref[...])
        _layernorm_store(y, g_ref, b_ref, of_ref, ob_ref, rows)


def _merge_out_ln(ys, proj, cols, b_gate, w_branch, w_out, x, g, b, alpha, tm):
    m, c = ys[0].shape
    nb, _, d = w_branch.shape
    gcol = cols["gate"] // (nb * d)
    yspec = pl.BlockSpec((tm, c), lambda i: (i, 0))
    row = pl.BlockSpec((tm, d), lambda i: (i, 0))
    vec = pl.BlockSpec((1, d), lambda i: (0, 0))
    return pl.pallas_call(
        functools.partial(_merge_out_ln_kernel, alpha=alpha),
        grid=(m // tm,),
        in_specs=[yspec, yspec, yspec,
                  pl.BlockSpec((tm, nb * d), lambda i: (i, gcol)),
                  pl.BlockSpec((1, nb * d), lambda i: (0, 0)),
                  pl.BlockSpec((nb, c, d), lambda i: (0, 0, 0), pipeline_mode=pl.Buffered(1)),
                  pl.BlockSpec((d, d), lambda i: (0, 0), pipeline_mode=pl.Buffered(1)),
                  row, vec, vec],
        out_specs=[row, row],
        out_shape=[jax.ShapeDtypeStruct((m, d), F32), jax.ShapeDtypeStruct((m, d), BF16)],
        compiler_params=_params("parallel"),
        name="merge_out_ln",
    )(*ys, proj["gate"], b_gate, w_branch, w_out, x, g, b)


def _mlp_kernel(xb_ref, xf_ref, w1_ref, b1_ref, w2_ref, b2_ref, g_ref, b_ref, of_ref, ob_ref, acc_ref,
                *, alpha):
    f = pl.program_id(1)

    @pl.when(f == 0)
    def _():
        acc_ref[...] = jnp.zeros(acc_ref.shape, F32)

    rb = xb_ref.shape[0] // ROW_BLOCKS
    hidden = []
    for h in range(ROW_BLOCKS):
        a = jnp.maximum(_dot(xb_ref[h * rb:(h + 1) * rb, :], w1_ref[...]) + b1_ref[...], 0.0)
        hidden.append((a * a).astype(BF16))
    for h in range(ROW_BLOCKS):
        acc_ref[h * rb:(h + 1) * rb, :] += _dot(hidden[h], w2_ref[...])

    @pl.when(f == pl.num_programs(1) - 1)
    def _():
        y = alpha * xf_ref[...] + (acc_ref[...] + b2_ref[...])
        _layernorm_store(y, g_ref, b_ref, of_ref, ob_ref)


def _mlp_ln(xb, xf, w1, b1, w2, b2, g, b, alpha, tm, tf):
    m, d = xf.shape
    ff = w1.shape[1]
    row = pl.BlockSpec((tm, d), lambda i, f: (i, 0))
    vec = pl.BlockSpec((1, d), lambda i, f: (0, 0))
    return pl.pallas_call(
        functools.partial(_mlp_kernel, alpha=alpha),
        grid=(m // tm, ff // tf),
        in_specs=[row, row,
                  pl.BlockSpec((d, tf), lambda i, f: (0, f)),
                  pl.BlockSpec((1, tf), lambda i, f: (0, f)),
                  pl.BlockSpec((tf, d), lambda i, f: (f, 0)),
                  vec, vec, vec],
        out_specs=[row, row],
        out_shape=[jax.ShapeDtypeStruct((m, d), F32), jax.ShapeDtypeStruct((m, d), BF16)],
        scratch_shapes=[pltpu.VMEM((tm, d), F32)],
        compiler_params=_params("parallel", "arbitrary"),
        name="mlp_ln",
    )(xb, xf, w1, b1, w2, b2, g, b)


def _projection_groups(names, sizes, src):
    groups = []
    for name in names:
        if sizes[name] < LANES:
            continue
        if groups:
            start, width, members = groups[-1]
            if src[name] == start + width and width % sizes[name] == 0:
                members[name] = width
                groups[-1] = (start, width + sizes[name], members)
                continue
        groups.append((src[name], sizes[name], {name: 0}))
    return groups


def _group_block_diag(w, width=2 * LANES):
    l, g, n, _ = w.shape
    per = width // n
    eye = jnp.eye(per, dtype=w.dtype)
    wg = w.reshape(l, g // per, per, n, n)
    return (eye[None, None, :, None, :, None] * wg[:, :, :, :, None, :]).reshape(l, g // per, width, width)


def kernel(x, w_in, lru_conv_w, lru_conv_b, lru_wa, lru_ba, lru_wi, lru_bi, lru_lambda, gla_wa2, gla_ba2, gla_norm_g, dn_conv_w, dn_a_log, dn_dt_bias, dn_norm_g, w_branch, b_gate, w_out, ln1_g, ln1_b, mlp_w1, mlp_b1, mlp_w2, mlp_b2, ln2_g, ln2_b):
    bsz, seq, d = x.shape
    depth = w_in.shape[0]
    m = bsz * seq
    alpha = (2.0 * depth) ** 0.25

    lru_w = lru_conv_w.shape[-1]
    gla_heads = 4
    gla_rank, gla_hdk = gla_wa2.shape[1], gla_wa2.shape[2]
    gla_hdv = gla_heads * gla_norm_g.shape[-1]
    dn_heads = dn_a_log.shape[-1]
    dn_cq = dn_conv_w.shape[-1]
    dn_hdv = dn_heads * dn_norm_g.shape[-1]
    nb = w_branch.shape[1]
    sizes = {"lru_x": lru_w, "lru_y": lru_w, "gla_q": gla_hdk, "gla_k": gla_hdk, "gla_v": gla_hdv,
             "gla_alr": gla_rank, "gla_r": gla_hdv, "dn_qkv": dn_cq, "dn_b": dn_heads, "dn_a": dn_heads,
             "dn_z": dn_hdv, "gate": nb * d}
    ref_order = ("lru_x", "lru_y", "gla_q", "gla_k", "gla_v", "gla_alr", "gla_r",
                 "dn_qkv", "dn_b", "dn_a", "dn_z", "gate")
    src, off = {}, 0
    for name in ref_order:
        src[name] = off
        off += sizes[name]
    assert off == w_in.shape[-1]
    take = lambda name: w_in[:, :, src[name]:src[name] + sizes[name]]
    groups = _projection_groups(ref_order, sizes, src)
    w_groups = [w_in[:, :, start:start + width].astype(BF16) for start, width, _ in groups]
    cols = {name: off for _, _, members in groups for name, off in members.items()}
    b_lane, a_lane = gla_rank, gla_rank + dn_heads
    n_small = gla_rank + 2 * dn_heads
    w_small = jnp.concatenate([take("gla_alr"), take("dn_b"), take("dn_a"),
                               jnp.zeros((depth, d, LANES - n_small), w_in.dtype)], axis=-1).astype(BF16)
    wa2_pad = jnp.concatenate([gla_wa2, jnp.zeros((depth, LANES - gla_rank, gla_hdk), gla_wa2.dtype)],
                              axis=1).astype(BF16)

    def lane_pad(p, lane):
        return jnp.pad(p, ((0, 0), (lane, LANES - lane - p.shape[1])))[:, None, :]

    alog_pad = lane_pad(dn_a_log, a_lane)
    dtb_pad = lane_pad(dn_dt_bias, a_lane)
    wa_grp = _group_block_diag(lru_wa).astype(BF16)
    wi_grp = _group_block_diag(lru_wi).astype(BF16)
    wb16, wo16 = w_branch.astype(BF16), w_out.astype(BF16)
    w1_16, w2_16 = mlp_w1.astype(BF16), mlp_w2.astype(BF16)
    vec = lambda p, l: p[l][None, :]

    tile = min(256, seq)
    tile_wide = min(512, seq)
    tm_proj = min(2048, m)
    xf = x.reshape(m, d)
    xb = xf.astype(BF16)
    for l in range(depth):
        proj = {}
        for (_, width, members), wg in zip(groups, w_groups):
            out = _matmul(xb, wg, l, BF16, tm_proj, min(1024, width), "in_proj")
            proj.update({name: out for name in members})
        small = _matmul(xb, w_small, l, F32, tm_proj, LANES, "in_proj_small")
        y_lru = _lru_branch(proj, cols, bsz, seq, lru_conv_w[l], vec(lru_conv_b, l), wa_grp[l],
                            vec(lru_ba, l), wi_grp[l], vec(lru_bi, l), vec(lru_lambda, l), tile_wide)
        y_gla = _gla_branch(proj, small, cols, bsz, seq, wa2_pad[l], vec(gla_ba2, l), vec(gla_norm_g, l), tile_wide)
        y_dn = _dn_branch(proj, small, cols, bsz, seq, dn_conv_w[l], alog_pad[l], dtb_pad[l],
                          vec(dn_norm_g, l), dn_heads, b_lane, a_lane, tile)
        xf, xb = _merge_out_ln((y_lru, y_gla, y_dn), proj, cols, b_gate[l].reshape(1, nb * d), wb16[l], wo16[l],
                               xf, vec(ln1_g, l), vec(ln1_b, l), alpha, min(256, m))
        xf, xb = _mlp_ln(xb, xf, w1_16[l], vec(mlp_b1, l), w2_16[l], vec(mlp_b2, l),
                         vec(ln2_g, l), vec(ln2_b, l), alpha, min(512, m), 1024)
    return xf.reshape(bsz, seq, d)
```
